```python
import jax, jax.numpy as jnp
from jax import lax
import numpy as np

D_MODEL = 1024
BATCH = 8
SEQ = 2048
DEPTH = 4

N_HEADS = 8
N_KV_HEADS = 2
HEAD_DIM = 128
Q_GROUP = N_HEADS // N_KV_HEADS
WINDOW = 128
BLOCK = 128
ROPE_THETA = 500000.0
ROPE_DIM = HEAD_DIM // 4
D_RNN = D_MODEL
N_RNN_BLOCKS = 4
RNN_BLOCK_W = D_RNN // N_RNN_BLOCKS
RNN_CONV_W = 4
RG_C = 8.0
D_FF = 3 * D_MODEL
FFN_CONV_W = 3
D_PLE = 256
LN_EPS = 1e-5
DN_ALPHA = (2 * DEPTH) ** 0.25
DN_BETA = (8 * DEPTH) ** -0.25

Q_WIDTH = N_HEADS * HEAD_DIM
KV_WIDTH = N_KV_HEADS * HEAD_DIM
IN_SIZES = (Q_WIDTH, KV_WIDTH, KV_WIDTH, D_RNN, D_RNN, D_MODEL, D_MODEL)
D_IN = sum(IN_SIZES)

kernel_name = "hybrid_swa_rglru_bidir_encoder"


def layer_norm(x, g, b):
    xf = x.astype(jnp.float32)
    mu = jnp.mean(xf, axis=-1, keepdims=True)
    var = jnp.mean(jnp.square(xf - mu), axis=-1, keepdims=True)
    y = (xf - mu) * lax.rsqrt(var + LN_EPS) * g.astype(jnp.float32) + b.astype(jnp.float32)
    return y.astype(x.dtype)


def rope_tables(seq_len, dtype):
    pos = jnp.arange(seq_len, dtype=jnp.float32)
    inv = ROPE_THETA ** (-jnp.arange(0, ROPE_DIM, 2, dtype=jnp.float32) / ROPE_DIM)
    ang = pos[:, None] * inv[None, :]
    return jnp.cos(ang).astype(dtype), jnp.sin(ang).astype(dtype)


def partial_rope(x, cos, sin):
    half = ROPE_DIM // 2
    x1, x2, rest = x[..., :half], x[..., half:ROPE_DIM], x[..., ROPE_DIM:]
    c = cos[None, :, None, :]
    s = sin[None, :, None, :]
    return jnp.concatenate([x1 * c - x2 * s, x2 * c + x1 * s, rest], axis=-1)


def band_blocks(t):
    bsz, s = t.shape[:2]
    nb = s // BLOCK
    tp = jnp.pad(t, ((0, 0), (BLOCK, BLOCK), (0, 0), (0, 0)))
    tb = tp.reshape(bsz, nb + 2, BLOCK, *t.shape[2:])
    return jnp.concatenate([tb[:, :-2], tb[:, 1:-1], tb[:, 2:]], axis=2)


def local_attention(q, k, v, sink):
    bsz, s = q.shape[:2]
    nb = s // BLOCK
    qb = q.reshape(bsz, nb, BLOCK, N_KV_HEADS, Q_GROUP, HEAD_DIM)
    kb = band_blocks(k)
    vb = band_blocks(v)
    logits = jnp.einsum('bnqhgd,bnkhd->bnhgqk', qb, kb,
                        preferred_element_type=jnp.float32) * (HEAD_DIM ** -0.5)
    qi = jnp.arange(BLOCK)[:, None]
    kj = jnp.arange(3 * BLOCK)[None, :]
    band = jnp.abs(kj - BLOCK - qi) <= WINDOW
    kpos = jnp.arange(nb)[:, None] * BLOCK - BLOCK + jnp.arange(3 * BLOCK)[None, :]
    in_range = (kpos >= 0) & (kpos < s)
    valid = band[None] & in_range[:, None, :]
    logits = jnp.where(valid[None, :, None, None], logits, -1e30)
    sink_col = jnp.broadcast_to(
        sink.astype(jnp.float32).reshape(1, 1, N_KV_HEADS, Q_GROUP, 1, 1),
        logits.shape[:-1] + (1,))
    probs = jax.nn.softmax(jnp.concatenate([logits, sink_col], axis=-1), axis=-1)[..., :-1]
    out = jnp.einsum('bnhgqk,bnkhd->bnqhgd', probs.astype(v.dtype), vb)
    return out.reshape(bsz, s, Q_WIDTH)


def directional_conv(x, w, b, reverse):
    kw = w.shape[0]
    s = x.shape[1]
    if reverse:
        xp = jnp.pad(x, ((0, 0), (0, kw - 1), (0, 0)))
        y = sum(w[k] * xp[:, k:k + s] for k in range(kw))
    else:
        xp = jnp.pad(x, ((0, 0), (kw - 1, 0), (0, 0)))
        y = sum(w[k] * xp[:, kw - 1 - k:kw - 1 - k + s] for k in range(kw))
    return y + b


def centred_conv(x, w, b):
    kw = w.shape[0]
    s = x.shape[1]
    pad = (kw - 1) // 2
    xp = jnp.pad(x, ((0, 0), (pad, pad), (0, 0)))
    return sum(w[k] * xp[:, k:k + s] for k in range(kw)) + b


def rg_lru(x, w_a, b_a, w_x, b_x, lam, reverse):
    bsz, s, c = x.shape
    xb = x.reshape(bsz, s, N_RNN_BLOCKS, RNN_BLOCK_W)
    gate_a = jax.nn.sigmoid(jnp.einsum('bsnc,ncd->bsnd', xb, w_a).reshape(bsz, s, c) + b_a)
    gate_x = jax.nn.sigmoid(jnp.einsum('bsnc,ncd->bsnd', xb, w_x).reshape(bsz, s, c) + b_x)
    log_a = -RG_C * gate_a.astype(jnp.float32) * jax.nn.softplus(-lam.astype(jnp.float32))
    a = jnp.exp(log_a)
    mult = jnp.sqrt(-jnp.expm1(2.0 * log_a))
    start = jnp.arange(s) == (s - 1 if reverse else 0)
    mult = jnp.where(start[None, :, None], 1.0, mult)
    u = x.astype(jnp.float32) * gate_x.astype(jnp.float32) * mult

    def combine(left, right):
        a_l, b_l = left
        a_r, b_r = right
        return a_l * a_r, a_r * b_l + b_r

    _, h = lax.associative_scan(combine, (a, u), reverse=reverse, axis=1)
    return h.astype(x.dtype)


def setup_inputs(seed: int = 0) -> dict:
    key = jax.random.key(seed)
    ks = jax.random.split(key, 32)
    f32 = jnp.float32
    L = DEPTH

    def nrm(k, shape, scale):
        return jax.random.normal(k, shape, f32) * scale

    rad = jax.random.uniform(ks[10], (L, 2, D_RNN), f32, 0.9, 0.999)
    return {
        "x": jax.random.normal(ks[0], (BATCH, SEQ, D_MODEL), f32),
        "p": jax.random.normal(ks[1], (DEPTH, BATCH, SEQ, D_PLE), f32),
        "ln_in_g": 1.0 + nrm(ks[2], (D_MODEL,), 0.02),
        "ln_in_b": nrm(ks[3], (D_MODEL,), 0.02),
        "w_in": nrm(ks[4], (L, D_MODEL, D_IN), D_MODEL ** -0.5),
        "attn_sink": nrm(ks[5], (L, N_HEADS), 0.5),
        "rnn_conv_w": nrm(ks[6], (L, 2, RNN_CONV_W, D_RNN), RNN_CONV_W ** -0.5),
        "rnn_conv_b": nrm(ks[7], (L, 2, D_RNN), 0.02),
        "rg_w_a": nrm(ks[8], (L, 2, N_RNN_BLOCKS, RNN_BLOCK_W, RNN_BLOCK_W), RNN_BLOCK_W ** -0.5),
        "rg_b_a": nrm(ks[9], (L, 2, D_RNN), 0.02),
        "rg_w_x": nrm(ks[11], (L, 2, N_RNN_BLOCKS, RNN_BLOCK_W, RNN_BLOCK_W), RNN_BLOCK_W ** -0.5),
        "rg_b_x": nrm(ks[12], (L, 2, D_RNN), 0.02),
        "rg_lambda": jnp.log(rad) - jnp.log1p(-rad),
        "w_proj_attn": nrm(ks[13], (L, Q_WIDTH, D_MODEL), Q_WIDTH ** -0.5),
        "w_proj_rnn": nrm(ks[14], (L, D_RNN, D_MODEL), D_RNN ** -0.5),
        "w_out": nrm(ks[15], (L, D_MODEL, D_MODEL), DN_BETA * D_MODEL ** -0.5),
        "ln1_g": 1.0 + nrm(ks[16], (L, D_MODEL), 0.02),
        "ln1_b": nrm(ks[17], (L, D_MODEL), 0.02),
        "w_up": nrm(ks[18], (L, D_MODEL, 2 * D_FF), D_MODEL ** -0.5),
        "ffn_conv_w": nrm(ks[19], (L, FFN_CONV_W, D_FF), FFN_CONV_W ** -0.5),
        "ffn_conv_b": nrm(ks[20], (L, D_FF), 0.02),
        "w_down": nrm(ks[21], (L, D_FF, D_MODEL), DN_BETA * D_FF ** -0.5),
        "w_ple": nrm(ks[22], (L, D_PLE, D_MODEL), DN_BETA * D_PLE ** -0.5),
        "w_ple_gate": nrm(ks[23], (L, D_MODEL, D_MODEL), D_MODEL ** -0.5),
        "ln2_g": 1.0 + nrm(ks[24], (L, D_MODEL), 0.02),
        "ln2_b": nrm(ks[25], (L, D_MODEL), 0.02),
    }


def reference(x, p, ln_in_g, ln_in_b, w_in, attn_sink, rnn_conv_w, rnn_conv_b,
              rg_w_a, rg_b_a, rg_w_x, rg_b_x, rg_lambda, w_proj_attn, w_proj_rnn,
              w_out, ln1_g, ln1_b, w_up, ffn_conv_w, ffn_conv_b, w_down,
              w_ple, w_ple_gate, ln2_g, ln2_b):
    bsz, s, _ = x.shape
    cos, sin = rope_tables(s, x.dtype)
    split_points = np.cumsum(IN_SIZES)[:-1].tolist()
    h = layer_norm(x, ln_in_g, ln_in_b)
    for i in range(DEPTH):
        z = h @ w_in[i]
        q, k, v, xr, yr, ga, gr = jnp.split(z, split_points, axis=-1)
        q = partial_rope(q.reshape(bsz, s, N_HEADS, HEAD_DIM), cos, sin)
        k = partial_rope(k.reshape(bsz, s, N_KV_HEADS, HEAD_DIM), cos, sin)
        v = v.reshape(bsz, s, N_KV_HEADS, HEAD_DIM)
        attn = local_attention(q, k, v, attn_sink[i])

        xf = directional_conv(xr, rnn_conv_w[i, 0], rnn_conv_b[i, 0], False)
        xb = directional_conv(xr, rnn_conv_w[i, 1], rnn_conv_b[i, 1], True)
        hr = (rg_lru(xf, rg_w_a[i, 0], rg_b_a[i, 0], rg_w_x[i, 0], rg_b_x[i, 0], rg_lambda[i, 0], False)
              + rg_lru(xb, rg_w_a[i, 1], rg_b_a[i, 1], rg_w_x[i, 1], rg_b_x[i, 1], rg_lambda[i, 1], True))
        rnn = hr * jax.nn.gelu(yr)

        merged = (jax.nn.sigmoid(ga) * (attn @ w_proj_attn[i])
                  + jax.nn.sigmoid(gr) * (rnn @ w_proj_rnn[i]))
        h = layer_norm(DN_ALPHA * h + merged @ w_out[i], ln1_g[i], ln1_b[i])

        gate, val = jnp.split(h @ w_up[i], 2, axis=-1)
        gate = centred_conv(gate, ffn_conv_w[i], ffn_conv_b[i])
        ffn = (jax.nn.gelu(gate) * val) @ w_down[i]
        ple = jax.nn.sigmoid(h @ w_ple_gate[i]) * (p[i] @ w_ple[i])
        h = layer_norm(DN_ALPHA * h + ffn + ple, ln2_g[i], ln2_b[i])
    return h
```

```python
import functools

import jax
import jax.numpy as jnp
from jax import lax
from jax.experimental import pallas as pl
from jax.experimental.pallas import tpu as pltpu

F32 = jnp.float32
BF16 = jnp.bfloat16

N_HEADS = 8
N_KV_HEADS = 2
HEAD_DIM = 128
Q_GROUP = N_HEADS // N_KV_HEADS
WINDOW = 128
ROPE_THETA = 500000.0
ROPE_DIM = HEAD_DIM // 4
N_RNN_BLOCKS = 4
RNN_CONV_W = 4
FFN_CONV_W = 3
RG_C = 8.0
LN_EPS = 1e-5
NEG_INF = -1e30

LANES = 128
SUBLANES = 8
VMEM_LIMIT = 56 * 1024 * 1024


def _cparams(*sem):
    return pltpu.CompilerParams(dimension_semantics=sem, vmem_limit_bytes=VMEM_LIMIT)


def _layer_norm(x, g, b):
    mu = jnp.mean(x, axis=-1, keepdims=True)
    xc = x - mu
    var = jnp.mean(xc * xc, axis=-1, keepdims=True)
    return xc * lax.rsqrt(var + LN_EPS) * g + b


def _gelu(x):
    return 0.5 * x * (1.0 + jnp.tanh(0.7978845608028654 * (x + 0.044715 * (x * x * x))))


def _softplus(x):
    return jnp.maximum(x, 0.0) + jnp.log1p(jnp.exp(-jnp.abs(x)))


def _const_spec(shape):
    nd = len(shape)
    return pl.BlockSpec(shape, lambda *_: (0,) * nd, pipeline_mode=pl.Buffered(1))


def _ln_in_kernel(x_ref, g_ref, b_ref, h_ref, hb_ref):
    y = _layer_norm(x_ref[...], g_ref[...], b_ref[...])
    h_ref[...] = y
    hb_ref[...] = y.astype(BF16)


def _ln_in(x2d, g, b, tm):
    m, d = x2d.shape
    return pl.pallas_call(
        _ln_in_kernel,
        grid=(m // tm,),
        in_specs=[pl.BlockSpec((tm, d), lambda i: (i, 0)),
                  _const_spec((1, d)), _const_spec((1, d))],
        out_specs=[pl.BlockSpec((tm, d), lambda i: (i, 0)),
                   pl.BlockSpec((tm, d), lambda i: (i, 0))],
        out_shape=[jax.ShapeDtypeStruct((m, d), F32), jax.ShapeDtypeStruct((m, d), BF16)],
        compiler_params=_cparams("parallel"),
        name="ln_in",
    )(x2d, g.reshape(1, d), b.reshape(1, d))


def _qkv_kernel(h_ref, w_ref, c_ref, s1_ref, s2_ref, o_ref, *, n_rope_last):
    j = pl.program_id(1)
    last = pl.num_programs(1) - 1
    z = jnp.dot(h_ref[...], w_ref[...], preferred_element_type=F32)
    c, s1, s2 = c_ref[...], s1_ref[...], s2_ref[...]
    half = ROPE_DIM // 2

    def rope(zc):
        return zc * c + pltpu.roll(zc, half, 1) * s1 + pltpu.roll(zc, LANES - half, 1) * s2

    for cc in range(z.shape[1] // LANES):
        sl = slice(cc * LANES, (cc + 1) * LANES)
        zc = z[:, sl]
        if cc < n_rope_last:
            o_ref[:, sl] = rope(zc).astype(BF16)
        else:
            @pl.when(j < last)
            def _():
                o_ref[:, sl] = rope(zc).astype(BF16)

            @pl.when(j == last)
            def _():
                o_ref[:, sl] = zc.astype(BF16)


def _qkv_proj(hb, w, rope_c, rope_s1, rope_s2, seq, tm, tn):
    m, d = hb.shape
    n = w.shape[1]
    n_rope_cols = (N_HEADS + N_KV_HEADS) * HEAD_DIM
    assert n % tn == 0 and (n - tn) <= n_rope_cols <= n
    n_rope_last = (n_rope_cols - (n - tn)) // LANES
    tiles_per_seq = seq // tm
    tab = pl.BlockSpec((tm, LANES), lambda i, j: (i % tiles_per_seq, 0))
    return pl.pallas_call(
        functools.partial(_qkv_kernel, n_rope_last=n_rope_last),
        grid=(m // tm, n // tn),
        in_specs=[pl.BlockSpec((tm, d), lambda i, j: (i, 0)),
                  pl.BlockSpec((d, tn), lambda i, j: (0, j)),
                  tab, tab, tab],
        out_specs=pl.BlockSpec((tm, tn), lambda i, j: (i, j)),
        out_shape=jax.ShapeDtypeStruct((m, n), BF16),
        compiler_params=_cparams("parallel", "arbitrary"),
        name="qkv_proj",
    )(hb, w, rope_c, rope_s1, rope_s2)


def _proj_kernel(h_ref, w_ref, o_ref):
    o_ref[...] = jnp.dot(h_ref[...], w_ref[...], preferred_element_type=F32)


def _rest_proj(hb, w, tm, tn):
    m, d = hb.shape
    n = w.shape[1]
    return pl.pallas_call(
        _proj_kernel,
        grid=(m // tm, n // tn),
        in_specs=[pl.BlockSpec((tm, d), lambda i, j: (i, 0)),
                  pl.BlockSpec((d, tn), lambda i, j: (0, j))],
        out_specs=pl.BlockSpec((tm, tn), lambda i, j: (i, j)),
        out_shape=jax.ShapeDtypeStruct((m, n), F32),
        compiler_params=_cparams("parallel", "arbitrary"),
        name="rest_proj",
    )(hb, w)


def _rnn_kernel(xf_ref, xb_ref, cw_ref, cb_ref, wg_ref, ba_ref, bx_ref, lam_ref,
                hf_ref, hb_ref,
                af_s, uf_s, ab_s, ub_s, tail_s, head_s, carry_s, *, pitch):
    i = pl.program_id(1)
    nb, ts, bw = xf_ref.shape
    nslab = bw // LANES

    @pl.when(i == 0)
    def _():
        tail_s[...] = jnp.zeros_like(tail_s)
        head_s[...] = jnp.zeros_like(head_s)
        carry_s[...] = jnp.zeros_like(carry_s)

    t_idx = lax.broadcasted_iota(jnp.int32, (nb, ts, bw), 1)

    def gates(d, xc, start_mask, a_s, u_s):
        g = jnp.dot(xc.reshape(nb * ts, bw).astype(BF16), wg_ref[d, 0],
                    preferred_element_type=F32).reshape(nb, ts, 2 * bw)
        gate_a = jax.nn.sigmoid(g[:, :, :bw] + ba_ref[d])
        gate_x = jax.nn.sigmoid(g[:, :, bw:] + bx_ref[d])
        log_a = (-RG_C) * gate_a * _softplus(-lam_ref[d])
        a = jnp.exp(log_a)
        mult = jnp.sqrt(jnp.tanh(-log_a) * (1.0 + a * a))
        mult = jnp.where(start_mask, 1.0, mult)
        u = xc * gate_x * mult
        for b in range(nb):
            for s in range(nslab):
                a_s[s, pl.ds(b * pitch, ts), :] = a[b, :, s * LANES:(s + 1) * LANES]
                u_s[s, pl.ds(b * pitch, ts), :] = u[b, :, s * LANES:(s + 1) * LANES]

    xf = xf_ref[...]
    xe = jnp.concatenate([tail_s[...], xf], axis=1)
    yf = cb_ref[0] + cw_ref[0, 0] * xf
    for k in range(1, RNN_CONV_W):
        yf = yf + cw_ref[0, k] * xe[:, SUBLANES - k:SUBLANES - k + ts, :]
    tail_s[...] = xf[:, ts - SUBLANES:, :]
    gates(0, yf, (t_idx == 0) & (i == 0), af_s, uf_s)

    xb = xb_ref[...]
    xe = jnp.concatenate([xb, head_s[...]], axis=1)
    yb = cb_ref[1] + cw_ref[1, 0] * xb
    for k in range(1, RNN_CONV_W):
        yb = yb + cw_ref[1, k] * xe[:, k:k + ts, :]
    head_s[...] = xb[:, :SUBLANES, :]
    gates(1, yb, (t_idx == ts - 1) & (i == 0), ab_s, ub_s)

    def step(tt, carry):
        new = []
        for d, (a_s, u_s) in enumerate(((af_s, uf_s), (ab_s, ub_s))):
            t = tt if d == 0 else ts - 1 - tt
            for s in range(nslab):
                rows = pl.ds(t, nb, stride=pitch)
                h = a_s[s, rows, :] * carry[d * nslab + s] + u_s[s, rows, :]
                u_s[s, rows, :] = h
                new.append(h)
        return tuple(new)

    init = tuple(carry_s[d, s] for d in range(2) for s in range(nslab))
    fin = lax.fori_loop(0, ts, step, init, unroll=8)
    for d in range(2):
        for s in range(nslab):
            carry_s[d, s] = fin[d * nslab + s]

    for b in range(nb):
        for s in range(nslab):
            hf_ref[b, :, s * LANES:(s + 1) * LANES] = uf_s[s, pl.ds(b * pitch, ts), :]
            hb_ref[b, :, s * LANES:(s + 1) * LANES] = ub_s[s, pl.ds(b * pitch, ts), :]


def _rnn(z3, conv_w, conv_b, wg, b_a, b_x, lam, ts):
    nb, seq, _ = z3.shape
    d = conv_w.shape[-1]
    bw = d // N_RNN_BLOCKS
    nt = seq // ts
    pitch = ts + SUBLANES if (ts // SUBLANES) % 2 == 0 else ts
    nslab = bw // LANES
    scan = pltpu.VMEM((nslab, nb * pitch, LANES), F32)
    vec = lambda: pl.BlockSpec((2, 1, bw), lambda c, i: (0, 0, c))
    return pl.pallas_call(
        functools.partial(_rnn_kernel, pitch=pitch),
        grid=(N_RNN_BLOCKS, nt),
        in_specs=[pl.BlockSpec((nb, ts, bw), lambda c, i: (0, i, c)),
                  pl.BlockSpec((nb, ts, bw), lambda c, i: (0, nt - 1 - i, c)),
                  pl.BlockSpec((2, RNN_CONV_W, 1, bw), lambda c, i: (0, 0, 0, c)),
                  vec(),
                  pl.BlockSpec((2, 1, bw, 2 * bw), lambda c, i: (0, c, 0, 0)),
                  vec(), vec(), vec()],
        out_specs=[pl.BlockSpec((nb, ts, bw), lambda c, i: (0, i, c)),
                   pl.BlockSpec((nb, ts, bw), lambda c, i: (0, nt - 1 - i, c))],
        out_shape=[jax.ShapeDtypeStruct((nb, seq, d), F32)] * 2,
        scratch_shapes=[scan, scan, scan, scan,
                        pltpu.VMEM((nb, SUBLANES, bw), F32),
                        pltpu.VMEM((nb, SUBLANES, bw), F32),
                        pltpu.VMEM((2, nslab, nb, LANES), F32)],
        compiler_params=_cparams("parallel", "arbitrary"),
        name="rg_lru",
    )(z3, z3, conv_w.reshape(2, RNN_CONV_W, 1, d), conv_b.reshape(2, 1, d), wg,
      b_a.reshape(2, 1, d), b_x.reshape(2, 1, d), lam.reshape(2, 1, d))


def _attn_kernel(sink_ref, q_ref, k_ref, v_ref, o_ref):
    hk = pl.program_id(1)
    seq = q_ref.shape[1]
    blk = WINDOW
    span = 3 * blk
    scale = HEAD_DIM ** -0.5

    def body(n, carry):
        q0 = pl.multiple_of(n * blk, blk)
        k0 = pl.multiple_of(jnp.clip((n - 1) * blk, 0, seq - span), blk)
        kw = k_ref[0, pl.ds(k0, span), :]
        vw = v_ref[0, pl.ds(k0, span), :]
        qpos = q0 + lax.broadcasted_iota(jnp.int32, (blk, span), 0)
        kpos = k0 + lax.broadcasted_iota(jnp.int32, (blk, span), 1)
        valid = jnp.abs(kpos - qpos) <= WINDOW
        for g in range(Q_GROUP):
            cols = slice(g * HEAD_DIM, (g + 1) * HEAD_DIM)
            qg = q_ref[0, pl.ds(q0, blk), cols]
            s = lax.dot_general(qg, kw, (((1,), (1,)), ((), ())),
                                preferred_element_type=F32) * scale
            s = jnp.where(valid, s, NEG_INF)
            sink = sink_ref[hk * Q_GROUP + g]
            m = jnp.maximum(jnp.max(s, axis=-1, keepdims=True), sink)
            p = jnp.exp(s - m)
            den = jnp.sum(p, axis=-1, keepdims=True) + jnp.exp(sink - m)
            o = jnp.dot(p.astype(BF16), vw, preferred_element_type=F32) / den
            o_ref[0, pl.ds(q0, blk), cols] = o.astype(BF16)
        return carry

    lax.fori_loop(0, seq // blk, body, 0)


def _attention(qkv3, sink):
    nb, seq, _ = qkv3.shape
    gw = Q_GROUP * HEAD_DIM
    k_blk0 = N_HEADS
    v_blk0 = N_HEADS + N_KV_HEADS
    return pl.pallas_call(
        _attn_kernel,
        grid=(nb, N_KV_HEADS),
        in_specs=[pl.BlockSpec(memory_space=pltpu.SMEM),
                  pl.BlockSpec((1, seq, gw), lambda b, h: (b, 0, h)),
                  pl.BlockSpec((1, seq, HEAD_DIM), lambda b, h: (b, 0, k_blk0 + h)),
                  pl.BlockSpec((1, seq, HEAD_DIM), lambda b, h: (b, 0, v_blk0 + h))],
        out_specs=pl.BlockSpec((1, seq, gw), lambda b, h: (b, 0, h)),
        out_shape=jax.ShapeDtypeStruct((nb, seq, N_HEADS * HEAD_DIM), BF16),
        compiler_params=_cparams("parallel", "parallel"),
        name="swa_attention",
    )(sink, qkv3, qkv3, qkv3)


def _merge_kernel(attn_ref, hf_ref, hbk_ref, yr_ref, ga_ref, gr_ref, h_ref,
                  wpa_ref, wpr_ref, wo_ref, g_ref, b_ref, o_ref, ob_ref, *, alpha):
    rnn = (hf_ref[...] + hbk_ref[...]) * _gelu(yr_ref[...])
    pa = jnp.dot(attn_ref[...], wpa_ref[...], preferred_element_type=F32)
    pr = jnp.dot(rnn.astype(BF16), wpr_ref[...], preferred_element_type=F32)
    merged = jax.nn.sigmoid(ga_ref[...]) * pa + jax.nn.sigmoid(gr_ref[...]) * pr
    o = jnp.dot(merged.astype(BF16), wo_ref[...], preferred_element_type=F32)
    y = _layer_norm(alpha * h_ref[...] + o, g_ref[...], b_ref[...])
    o_ref[...] = y
    ob_ref[...] = y.astype(BF16)


def _merge(attn, hf, hbk, zr, h, wpa, wpr, wo, g, b, alpha, tm):
    m, d = h.shape
    row = lambda c: pl.BlockSpec((tm, d), lambda i: (i, c))
    return pl.pallas_call(
        functools.partial(_merge_kernel, alpha=alpha),
        grid=(m // tm,),
        in_specs=[row(0), row(0), row(0), row(1), row(2), row(3), row(0),
                  _const_spec((d, d)), _const_spec((d, d)), _const_spec((d, d)),
                  _const_spec((1, d)), _const_spec((1, d))],
        out_specs=[row(0), row(0)],
        out_shape=[jax.ShapeDtypeStruct((m, d), F32), jax.ShapeDtypeStruct((m, d), BF16)],
        compiler_params=_cparams("parallel"),
        name="merge_out_ln",
    )(attn, hf, hbk, zr, zr, zr, h, wpa, wpr, wo, g.reshape(1, d), b.reshape(1, d))


def _ffn_kernel(hb_ref, hprev_ref, hnext_ref, h_ref, p_ref, wg_ref, wv_ref, wd_ref,
                cw_ref, cb_ref, wpg_ref, wple_ref, g_ref, b_ref, o_ref, ob_ref, acc_ref,
                *, alpha, tiles_per_seq):
    i = pl.program_id(0)
    j = pl.program_id(1)
    tm = hb_ref.shape[0]
    hb = hb_ref[...]
    wg = wg_ref[...]
    gate = jnp.dot(hb, wg, preferred_element_type=F32)
    val = jnp.dot(hb, wv_ref[...], preferred_element_type=F32)
    g_before = jnp.dot(hprev_ref[...], wg, preferred_element_type=F32)[SUBLANES - 1:SUBLANES, :]
    g_after = jnp.dot(hnext_ref[...], wg, preferred_element_type=F32)[0:1, :]
    pos = i % tiles_per_seq
    g_before = jnp.where(pos == 0, 0.0, g_before)
    g_after = jnp.where(pos == tiles_per_seq - 1, 0.0, g_after)
    row = lax.broadcasted_iota(jnp.int32, gate.shape, 0)
    g_m1 = jnp.where(row == 0, g_before, pltpu.roll(gate, 1, 0))
    g_p1 = jnp.where(row == tm - 1, g_after, pltpu.roll(gate, tm - 1, 0))
    conv = cw_ref[0:1, :] * g_m1 + cw_ref[1:2, :] * gate + cw_ref[2:3, :] * g_p1 + cb_ref[...]
    act = (_gelu(conv) * val).astype(BF16)
    part = jnp.dot(act, wd_ref[...], preferred_element_type=F32)

    @pl.when(j == 0)
    def _():
        acc_ref[...] = part

    @pl.when(j > 0)
    def _():
        acc_ref[...] += part

    @pl.when(j == pl.num_programs(1) - 1)
    def _():
        ple = (jax.nn.sigmoid(jnp.dot(hb, wpg_ref[...], preferred_element_type=F32))
               * jnp.dot(p_ref[...].astype(BF16), wple_ref[...], preferred_element_type=F32))
        y = _layer_norm(alpha * h_ref[...] + acc_ref[...] + ple, g_ref[...], b_ref[...])
        o_ref[...] = y
        ob_ref[...] = y.astype(BF16)


def _ffn(h, hb, p2d, w_up, w_down, cw, cb, wpg, wple, g, b, alpha, seq, tm, tf):
    m, d = h.shape
    dff = w_down.shape[0]
    dple = p2d.shape[1]
    nf = dff // tf
    rows8 = tm // SUBLANES
    last8 = m // SUBLANES - 1
    return pl.pallas_call(
        functools.partial(_ffn_kernel, alpha=alpha, tiles_per_seq=seq // tm),
        grid=(m // tm, nf),
        in_specs=[pl.BlockSpec((tm, d), lambda i, j: (i, 0)),
                  pl.BlockSpec((SUBLANES, d), lambda i, j: (jnp.maximum(i * rows8 - 1, 0), 0)),
                  pl.BlockSpec((SUBLANES, d), lambda i, j: (jnp.minimum((i + 1) * rows8, last8), 0)),
                  pl.BlockSpec((tm, d), lambda i, j: (i, 0)),
                  pl.BlockSpec((tm, dple), lambda i, j: (i, 0)),
                  pl.BlockSpec((d, tf), lambda i, j: (0, j)),
                  pl.BlockSpec((d, tf), lambda i, j: (0, nf + j)),
                  pl.BlockSpec((tf, d), lambda i, j: (j, 0)),
                  pl.BlockSpec((FFN_CONV_W, tf), lambda i, j: (0, j)),
                  pl.BlockSpec((1, tf), lambda i, j: (0, j)),
                  _const_spec((d, d)), _const_spec((dple, d)),
                  _const_spec((1, d)), _const_spec((1, d))],
        out_specs=[pl.BlockSpec((tm, d), lambda i, j: (i, 0)),
                   pl.BlockSpec((tm, d), lambda i, j: (i, 0))],
        out_shape=[jax.ShapeDtypeStruct((m, d), F32), jax.ShapeDtypeStruct((m, d), BF16)],
        scratch_shapes=[pltpu.VMEM((tm, d), F32)],
        compiler_params=_cparams("parallel", "arbitrary"),
        name="ffn_ple_ln",
    )(hb, hb, hb, h, p2d, w_up, w_up, w_down, cw, cb.reshape(1, dff), wpg, wple,
      g.reshape(1, d), b.reshape(1, d))


def _rope_tables(seq):
    half = ROPE_DIM // 2
    pos = jnp.arange(seq, dtype=F32)
    inv = ROPE_THETA ** (-jnp.arange(0, ROPE_DIM, 2, dtype=F32) / ROPE_DIM)
    ang = pos[:, None] * inv[None, :]
    cos, sin = jnp.cos(ang), jnp.sin(ang)
    zeros = jnp.zeros((seq, HEAD_DIM - ROPE_DIM), F32)
    zh = jnp.zeros((seq, half), F32)
    c = jnp.concatenate([cos, cos, zeros + 1.0], axis=1)
    s1 = jnp.concatenate([zh, sin, zeros], axis=1)
    s2 = jnp.concatenate([-sin, zh, zeros], axis=1)
    return c, s1, s2


def kernel(x, p, ln_in_g, ln_in_b, w_in, attn_sink, rnn_conv_w, rnn_conv_b, rg_w_a, rg_b_a, rg_w_x, rg_b_x, rg_lambda, w_proj_attn, w_proj_rnn, w_out, ln1_g, ln1_b, w_up, ffn_conv_w, ffn_conv_b, w_down, w_ple, w_ple_gate, ln2_g, ln2_b):
    nb, seq, d = x.shape
    depth = w_in.shape[0]
    m = nb * seq
    alpha = float((2 * depth) ** 0.25)
    n_qkv = (N_HEADS + 2 * N_KV_HEADS) * HEAD_DIM
    assert nb == SUBLANES and d == N_HEADS * HEAD_DIM and seq % WINDOW == 0 and seq >= 3 * WINDOW

    tm_proj = min(1024, seq)
    tm_merge = min(512, seq)
    tm_ffn = min(1024, seq)
    ts_rnn = min(256, seq)
    tn = 512

    rope_c, rope_s1, rope_s2 = _rope_tables(seq)
    h, hb = _ln_in(x.reshape(m, d), ln_in_g, ln_in_b, tm_proj)

    for l in range(depth):
        w_in_b = w_in[l].astype(BF16)
        qkv = _qkv_proj(hb, w_in_b[:, :n_qkv], rope_c, rope_s1, rope_s2, seq, tm_proj, tn)
        zr = _rest_proj(hb, w_in_b[:, n_qkv:], tm_proj, tn)

        wg = jnp.concatenate([rg_w_a[l], rg_w_x[l]], axis=-1).astype(BF16)
        hf, hbk = _rnn(zr.reshape(nb, seq, 4 * d), rnn_conv_w[l], rnn_conv_b[l], wg,
                       rg_b_a[l], rg_b_x[l], rg_lambda[l], ts_rnn)
        attn = _attention(qkv.reshape(nb, seq, n_qkv), attn_sink[l])

        h, hb = _merge(attn.reshape(m, d), hf.reshape(m, d), hbk.reshape(m, d), zr, h,
                       w_proj_attn[l].astype(BF16), w_proj_rnn[l].astype(BF16),
                       w_out[l].astype(BF16), ln1_g[l], ln1_b[l], alpha, tm_merge)

        h, hb = _ffn(h, hb, p[l].reshape(m, -1), w_up[l].astype(BF16), w_down[l].astype(BF16),
                     ffn_conv_w[l], ffn_conv_b[l], w_ple_gate[l].astype(BF16),
                     w_ple[l].astype(BF16), ln2_g[l], ln2_b[l], alpha, seq, tm_ffn, tn)
    return h.reshape(nb, seq, d)
```

```python
import functools
import math

import jax
import jax.numpy as jnp
from jax import lax
from jax.experimental import pallas as pl
from jax.experimental.pallas import tpu as pltpu

F32 = jnp.float32
BF16 = jnp.bfloat16

N_HEADS = 8
N_KV_HEADS = 2
HEAD_DIM = 128
Q_GROUP = N_HEADS // N_KV_HEADS
WINDOW = 128
ROPE_THETA = 500000.0
ROPE_DIM = HEAD_DIM // 4
N_RNN_BLOCKS = 4
RNN_CONV_W = 4
FFN_CONV_W = 3
RG_C = 8.0
LN_EPS = 1e-5
NEG_INF = -1e30
LOG2E = math.log2(math.e)

LANES = 128
SUBLANES = 8
BF16_ROWS = 16
MXU_COLS = 256
VMEM_LIMIT = 56 * 1024 * 1024

TM_PROJ = 2048
TM_MERGE = 512
TM_FFN = 1024
TS_RNN = 256
TN = 512


def _cparams(*sem):
    return pltpu.CompilerParams(dimension_semantics=sem, vmem_limit_bytes=VMEM_LIMIT)


def _layer_norm(x, g, b):
    mu = jnp.mean(x, axis=-1, keepdims=True)
    xc = x - mu
    var = jnp.mean(xc * xc, axis=-1, keepdims=True)
    return xc * lax.rsqrt(var + LN_EPS) * g + b


def _gelu(x):
    return 0.5 * x * (1.0 + jnp.tanh(0.7978845608028654 * (x + 0.044715 * (x * x * x))))


def _softplus(x):
    return jnp.maximum(x, 0.0) + jnp.log1p(jnp.exp(-jnp.abs(x)))


def _const_spec(shape):
    nd = len(shape)
    return pl.BlockSpec(shape, lambda *_: (0,) * nd, pipeline_mode=pl.Buffered(1))


def _ln_in_kernel(x_ref, g_ref, b_ref, h_ref, hb_ref):
    y = _layer_norm(x_ref[...], g_ref[...], b_ref[...])
    h_ref[...] = y
    hb_ref[...] = y.astype(BF16)


def _ln_in(x2d, g, b, tm):
    m, d = x2d.shape
    return pl.pallas_call(
        _ln_in_kernel,
        grid=(m // tm,),
        in_specs=[pl.BlockSpec((tm, d), lambda i: (i, 0)),
                  _const_spec((1, d)), _const_spec((1, d))],
        out_specs=[pl.BlockSpec((tm, d), lambda i: (i, 0)),
                   pl.BlockSpec((tm, d), lambda i: (i, 0))],
        out_shape=[jax.ShapeDtypeStruct((m, d), F32), jax.ShapeDtypeStruct((m, d), BF16)],
        compiler_params=_cparams("parallel"),
        name="ln_in",
    )(x2d, g.reshape(1, d), b.reshape(1, d))


def _qkv_kernel(h_ref, w_ref, c_ref, s1_ref, s2_ref, o_ref, *, n_rope_last):
    j = pl.program_id(1)
    last = pl.num_programs(1) - 1
    c, s1, s2 = c_ref[...], s1_ref[...], s2_ref[...]
    half = ROPE_DIM // 2
    h = h_ref[...]

    def rope(zc):
        return zc * c + pltpu.roll(zc, half, 1) * s1 + pltpu.roll(zc, LANES - half, 1) * s2

    heads_per_dot = MXU_COLS // LANES
    for cd in range(w_ref.shape[1] // MXU_COLS):
        z = jnp.dot(h, w_ref[:, cd * MXU_COLS:(cd + 1) * MXU_COLS], preferred_element_type=F32)
        for ch in range(heads_per_dot):
            cc = cd * heads_per_dot + ch
            sl = slice(cc * LANES, (cc + 1) * LANES)
            zc = z[:, ch * LANES:(ch + 1) * LANES]
            if cc < n_rope_last:
                o_ref[:, sl] = rope(zc).astype(BF16)
            else:
                @pl.when(j < last)
                def _():
                    o_ref[:, sl] = rope(zc).astype(BF16)

                @pl.when(j == last)
                def _():
                    o_ref[:, sl] = zc.astype(BF16)


def _qkv_proj(hb, w, rope_c, rope_s1, rope_s2, seq, tm, tn):
    m, d = hb.shape
    n = w.shape[1]
    n_rope_cols = (N_HEADS + N_KV_HEADS) * HEAD_DIM
    assert n % tn == 0 and tn % MXU_COLS == 0 and (n - tn) <= n_rope_cols <= n
    n_rope_last = (n_rope_cols - (n - tn)) // LANES
    tiles_per_seq = seq // tm
    tab = pl.BlockSpec((tm, LANES), lambda i, j: (i % tiles_per_seq, 0))
    return pl.pallas_call(
        functools.partial(_qkv_kernel, n_rope_last=n_rope_last),
        grid=(m // tm, n // tn),
        in_specs=[pl.BlockSpec((tm, d), lambda i, j: (i, 0)),
                  pl.BlockSpec((d, tn), lambda i, j: (0, j)),
                  tab, tab, tab],
        out_specs=pl.BlockSpec((tm, tn), lambda i, j: (i, j)),
        out_shape=jax.ShapeDtypeStruct((m, n), BF16),
        compiler_params=_cparams("parallel", "arbitrary"),
        name="qkv_proj",
    )(hb, w, rope_c, rope_s1, rope_s2)


def _proj_kernel(h_ref, w_ref, o_ref):
    o_ref[...] = jnp.dot(h_ref[...], w_ref[...], preferred_element_type=F32)


def _xr_proj(hb, w, tm, tn):
    m, d = hb.shape
    n = w.shape[1]
    return pl.pallas_call(
        _proj_kernel,
        grid=(m // tm, n // tn),
        in_specs=[pl.BlockSpec((tm, d), lambda i, j: (i, 0)),
                  pl.BlockSpec((d, tn), lambda i, j: (0, j))],
        out_specs=pl.BlockSpec((tm, tn), lambda i, j: (i, j)),
        out_shape=jax.ShapeDtypeStruct((m, n), F32),
        compiler_params=_cparams("parallel", "arbitrary"),
        name="xr_proj",
    )(hb, w)


def _rnn_kernel(xf_ref, xb_ref, cw_ref, cb_ref, wg_ref, ba_ref, bx_ref, lam_ref,
                hf_ref, hb_ref,
                xsf_s, xsb_s, af_s, uf_s, ab_s, ub_s, carry_s, *, pitch):
    i = pl.program_id(1)
    nb, ts, bw = xf_ref.shape
    nslab = bw // LANES
    halo = SUBLANES

    @pl.when(i == 0)
    def _():
        xsf_s[:, 0:halo, :] = jnp.zeros((nb, halo, bw), F32)
        xsb_s[:, ts:ts + halo, :] = jnp.zeros((nb, halo, bw), F32)
        carry_s[...] = jnp.zeros_like(carry_s)

    t_idx = lax.broadcasted_iota(jnp.int32, (nb, ts, bw), 1)

    def gates(d, xc, start_mask, a_s, u_s):
        g = jnp.dot(xc.reshape(nb * ts, bw).astype(BF16), wg_ref[d, 0],
                    preferred_element_type=F32).reshape(nb, ts, 2 * bw)
        gate_a = jax.nn.sigmoid(g[:, :, :bw] + ba_ref[d])
        gate_x = jax.nn.sigmoid(g[:, :, bw:] + bx_ref[d])
        log_a = gate_a * ((-RG_C) * _softplus(-lam_ref[d]))
        a = jnp.exp(log_a)
        mult = jnp.sqrt(jnp.tanh(log_a) * (-1.0 - a * a))
        mult = jnp.where(start_mask, 1.0, mult)
        u = xc * gate_x * mult
        for b in range(nb):
            for s in range(nslab):
                a_s[s, pl.ds(b * pitch, ts), :] = a[b, :, s * LANES:(s + 1) * LANES]
                u_s[s, pl.ds(b * pitch, ts), :] = u[b, :, s * LANES:(s + 1) * LANES]

    xsf_s[:, halo:halo + ts, :] = xf_ref[...]
    yf = cb_ref[0] + cw_ref[0, 0] * xsf_s[:, halo:halo + ts, :]
    for k in range(1, RNN_CONV_W):
        yf = yf + cw_ref[0, k] * xsf_s[:, halo - k:halo - k + ts, :]
    xsf_s[:, 0:halo, :] = xsf_s[:, ts:ts + halo, :]
    gates(0, yf, (t_idx == 0) & (i == 0), af_s, uf_s)

    xsb_s[:, 0:ts, :] = xb_ref[...]
    yb = cb_ref[1] + cw_ref[1, 0] * xsb_s[:, 0:ts, :]
    for k in range(1, RNN_CONV_W):
        yb = yb + cw_ref[1, k] * xsb_s[:, k:k + ts, :]
    xsb_s[:, ts:ts + halo, :] = xsb_s[:, 0:halo, :]
    gates(1, yb, (t_idx == ts - 1) & (i == 0), ab_s, ub_s)

    def step(tt, carry):
        new = []
        for d, (a_s, u_s) in enumerate(((af_s, uf_s), (ab_s, ub_s))):
            t = tt if d == 0 else ts - 1 - tt
            for s in range(nslab):
                rows = pl.ds(t, nb, stride=pitch)
                h = a_s[s, rows, :] * carry[d * nslab + s] + u_s[s, rows, :]
                u_s[s, rows, :] = h
                new.append(h)
        return tuple(new)

    init = tuple(carry_s[d, s] for d in range(2) for s in range(nslab))
    fin = lax.fori_loop(0, ts, step, init, unroll=8)
    for d in range(2):
        for s in range(nslab):
            carry_s[d, s] = fin[d * nslab + s]

    for b in range(nb):
        for s in range(nslab):
            hf_ref[b, :, s * LANES:(s + 1) * LANES] = uf_s[s, pl.ds(b * pitch, ts), :]
            hb_ref[b, :, s * LANES:(s + 1) * LANES] = ub_s[s, pl.ds(b * pitch, ts), :]


def _rnn(x3, conv_w, conv_b, wg, b_a, b_x, lam, ts):
    nb, seq, d = x3.shape
    bw = d // N_RNN_BLOCKS
    nt = seq // ts
    pitch = ts + SUBLANES if (ts // SUBLANES) % 2 == 0 else ts
    nslab = bw // LANES
    scan = pltpu.VMEM((nslab, nb * pitch, LANES), F32)
    xs = pltpu.VMEM((nb, ts + SUBLANES, bw), F32)
    vec = lambda: pl.BlockSpec((2, 1, bw), lambda c, i: (0, 0, c))
    return pl.pallas_call(
        functools.partial(_rnn_kernel, pitch=pitch),
        grid=(N_RNN_BLOCKS, nt),
        in_specs=[pl.BlockSpec((nb, ts, bw), lambda c, i: (0, i, c)),
                  pl.BlockSpec((nb, ts, bw), lambda c, i: (0, nt - 1 - i, c)),
                  pl.BlockSpec((2, RNN_CONV_W, 1, bw), lambda c, i: (0, 0, 0, c)),
                  vec(),
                  pl.BlockSpec((2, 1, bw, 2 * bw), lambda c, i: (0, c, 0, 0)),
                  vec(), vec(), vec()],
        out_specs=[pl.BlockSpec((nb, ts, bw), lambda c, i: (0, i, c)),
                   pl.BlockSpec((nb, ts, bw), lambda c, i: (0, nt - 1 - i, c))],
        out_shape=[jax.ShapeDtypeStruct((nb, seq, d), F32)] * 2,
        scratch_shapes=[xs, xs, scan, scan, scan, scan,
                        pltpu.VMEM((2, nslab, nb, LANES), F32)],
        compiler_params=_cparams("parallel", "arbitrary"),
        name="rg_lru",
    )(x3, x3, conv_w.reshape(2, RNN_CONV_W, 1, d), conv_b.reshape(2, 1, d), wg,
      b_a.reshape(2, 1, d), b_x.reshape(2, 1, d), lam.reshape(2, 1, d))


def _attn_kernel(sink_ref, q_ref, k_ref, v_ref, o_ref, kp_s, vp_s, s_s, p_s):
    hk = pl.program_id(1)
    seq = q_ref.shape[1]
    blk = WINDOW
    nblk = seq // blk
    span = 3 * blk
    c1 = HEAD_DIM ** -0.5 * LOG2E

    kp_s[0:blk, :] = jnp.zeros((blk, HEAD_DIM), BF16)
    kp_s[blk:blk + seq, :] = k_ref[0]
    kp_s[blk + seq:2 * blk + seq, :] = jnp.zeros((blk, HEAD_DIM), BF16)
    vp_s[0:blk, 0:HEAD_DIM] = jnp.zeros((blk, HEAD_DIM), BF16)
    vp_s[blk:blk + seq, 0:HEAD_DIM] = v_ref[0]
    vp_s[blk + seq:2 * blk + seq, 0:HEAD_DIM] = jnp.zeros((blk, HEAD_DIM), BF16)
    vp_s[:, HEAD_DIM:] = jnp.ones((seq + 2 * blk, HEAD_DIM), BF16)

    qi = lax.broadcasted_iota(jnp.int32, (blk, blk), 0)
    kj = lax.broadcasted_iota(jnp.int32, (blk, blk), 1)
    bias_prev = jnp.where(kj >= qi, 0.0, NEG_INF)
    bias_next = jnp.where(kj <= qi, 0.0, NEG_INF)

    def logits(n, slot):
        q0 = pl.multiple_of(n * blk, blk)
        qs = jnp.concatenate(
            [q_ref[0, pl.ds(q0, blk), g * HEAD_DIM:(g + 1) * HEAD_DIM] for g in range(Q_GROUP)], axis=0)
        s_s[slot] = lax.dot_general(qs, kp_s[pl.ds(q0, span), :], (((1,), (1,)), ((), ())),
                                    preferred_element_type=F32)

    logits(0, 0)

    def block(n, slot):
        logits(jnp.minimum(n + 1, nblk - 1), 1 - slot)
        q0 = pl.multiple_of(n * blk, blk)
        bp = jnp.where(n == 0, NEG_INF, bias_prev)
        bn = jnp.where(n == nblk - 1, NEG_INF, bias_next)
        m2s = []
        for g in range(Q_GROUP):
            r = slice(g * blk, (g + 1) * blk)
            s_prev = s_s[slot, r, 0:blk] + bp
            s_cur = s_s[slot, r, blk:2 * blk]
            s_next = s_s[slot, r, 2 * blk:span] + bn
            m_raw = jnp.max(jnp.maximum(jnp.maximum(s_prev, s_cur), s_next), axis=-1, keepdims=True)
            m2 = jnp.maximum(m_raw * c1, sink_ref[hk * Q_GROUP + g] * LOG2E)
            p_s[slot, r, 0:blk] = jnp.exp2(s_prev * c1 - m2).astype(BF16)
            p_s[slot, r, blk:2 * blk] = jnp.exp2(s_cur * c1 - m2).astype(BF16)
            p_s[slot, r, 2 * blk:span] = jnp.exp2(s_next * c1 - m2).astype(BF16)
            m2s.append(m2)
        oe = jnp.dot(p_s[slot], vp_s[pl.ds(q0, span), :], preferred_element_type=F32)
        for g in range(Q_GROUP):
            r = slice(g * blk, (g + 1) * blk)
            den = oe[r, HEAD_DIM:] + jnp.exp2(sink_ref[hk * Q_GROUP + g] * LOG2E - m2s[g])
            o_ref[0, pl.ds(q0, blk), g * HEAD_DIM:(g + 1) * HEAD_DIM] = (oe[r, :HEAD_DIM] / den).astype(BF16)

    def body(i, carry):
        block(2 * i, 0)
        block(2 * i + 1, 1)
        return carry

    assert nblk % 2 == 0
    lax.fori_loop(0, nblk // 2, body, 0)


def _attention(qkv3, sink):
    nb, seq, _ = qkv3.shape
    gw = Q_GROUP * HEAD_DIM
    k_blk0 = N_HEADS
    v_blk0 = N_HEADS + N_KV_HEADS
    rows = Q_GROUP * WINDOW
    return pl.pallas_call(
        _attn_kernel,
        grid=(nb, N_KV_HEADS),
        in_specs=[pl.BlockSpec(memory_space=pltpu.SMEM),
                  pl.BlockSpec((1, seq, gw), lambda b, h: (b, 0, h)),
                  pl.BlockSpec((1, seq, HEAD_DIM), lambda b, h: (b, 0, k_blk0 + h)),
                  pl.BlockSpec((1, seq, HEAD_DIM), lambda b, h: (b, 0, v_blk0 + h))],
        out_specs=pl.BlockSpec((1, seq, gw), lambda b, h: (b, 0, h)),
        out_shape=jax.ShapeDtypeStruct((nb, seq, N_HEADS * HEAD_DIM), BF16),
        scratch_shapes=[pltpu.VMEM((seq + 2 * WINDOW, HEAD_DIM), BF16),
                        pltpu.VMEM((seq + 2 * WINDOW, 2 * HEAD_DIM), BF16),
                        pltpu.VMEM((2, rows, 3 * WINDOW), F32),
                        pltpu.VMEM((2, rows, 3 * WINDOW), BF16)],
        compiler_params=_cparams("parallel", "parallel"),
        name="swa_attention",
    )(sink, qkv3, qkv3, qkv3)


def _merge_kernel(hb_ref, attn_ref, hf_ref, hbk_ref, h_ref,
                  wz_ref, wpa_ref, wpr_ref, wo_ref, g_ref, b_ref, o_ref, ob_ref, *, alpha):
    d = h_ref.shape[1]
    z = jnp.dot(hb_ref[...], wz_ref[...], preferred_element_type=F32)
    rnn = (hf_ref[...] + hbk_ref[...]) * _gelu(z[:, :d])
    pa = jnp.dot(attn_ref[...], wpa_ref[...], preferred_element_type=F32)
    pr = jnp.dot(rnn.astype(BF16), wpr_ref[...], preferred_element_type=F32)
    merged = jax.nn.sigmoid(z[:, d:2 * d]) * pa + jax.nn.sigmoid(z[:, 2 * d:]) * pr
    o = jnp.dot(merged.astype(BF16), wo_ref[...], preferred_element_type=F32)
    y = _layer_norm(alpha * h_ref[...] + o, g_ref[...], b_ref[...])
    o_ref[...] = y
    ob_ref[...] = y.astype(BF16)


def _merge(hb, attn, hf, hbk, h, wz, wpa, wpr, wo, g, b, alpha, tm):
    m, d = h.shape
    row = pl.BlockSpec((tm, d), lambda i: (i, 0))
    return pl.pallas_call(
        functools.partial(_merge_kernel, alpha=alpha),
        grid=(m // tm,),
        in_specs=[row, row, row, row, row,
                  _const_spec(wz.shape), _const_spec((d, d)), _const_spec((d, d)), _const_spec((d, d)),
                  _const_spec((1, d)), _const_spec((1, d))],
        out_specs=[row, row],
        out_shape=[jax.ShapeDtypeStruct((m, d), F32), jax.ShapeDtypeStruct((m, d), BF16)],
        compiler_params=_cparams("parallel"),
        name="merge_out_ln",
    )(hb, attn, hf, hbk, h, wz, wpa, wpr, wo, g.reshape(1, d), b.reshape(1, d))


def _ffn_kernel(hb_ref, hprev_ref, hnext_ref, h_ref, p_ref, wu_ref, wd_ref,
                cw_ref, cb_ref, wpg_ref, wple_ref, g_ref, b_ref, o_ref, ob_ref, acc_ref,
                *, alpha, tiles_per_seq):
    i = pl.program_id(0)
    j = pl.program_id(1)
    tm = hb_ref.shape[0]
    tf = wd_ref.shape[0]
    halo = hprev_ref.shape[0]
    hb = hb_ref[...]
    pos = i % tiles_per_seq
    hp = hprev_ref[...]
    hn = hnext_ref[...]
    hp = jnp.where(pos == 0, jnp.zeros_like(hp), hp)
    hn = jnp.where(pos == tiles_per_seq - 1, jnp.zeros_like(hn), hn)
    lhs = jnp.concatenate([hp, hb, hn], axis=0)
    part = None
    for c in range(tf // MXU_COLS):
        cs = slice(c * MXU_COLS, (c + 1) * MXU_COLS)
        gv = jnp.dot(lhs, wu_ref[:, 2 * c * MXU_COLS:2 * (c + 1) * MXU_COLS],
                     preferred_element_type=F32)
        gate = gv[:, :MXU_COLS]
        val = gv[halo:halo + tm, MXU_COLS:]
        conv = (cw_ref[0:1, cs] * gate[halo - 1:halo - 1 + tm] + cw_ref[1:2, cs] * gate[halo:halo + tm]
                + cw_ref[2:3, cs] * gate[halo + 1:halo + 1 + tm] + cb_ref[:, cs])
        act = (_gelu(conv) * val).astype(BF16)
        pc = jnp.dot(act, wd_ref[cs, :], preferred_element_type=F32)
        part = pc if part is None else part + pc

    @pl.when(j == 0)
    def _():
        acc_ref[...] = part

    @pl.when(j > 0)
    def _():
        acc_ref[...] += part

    @pl.when(j == pl.num_programs(1) - 1)
    def _():
        ple = (jax.nn.sigmoid(jnp.dot(hb, wpg_ref[...], preferred_element_type=F32))
               * jnp.dot(p_ref[...].astype(BF16), wple_ref[...], preferred_element_type=F32))
        y = _layer_norm(alpha * h_ref[...] + acc_ref[...] + ple, g_ref[...], b_ref[...])
        o_ref[...] = y
        ob_ref[...] = y.astype(BF16)


def _ffn(h, hb, p2d, w_up_tiled, w_down, cw, cb, wpg, wple, g, b, alpha, seq, tm, tf):
    m, d = h.shape
    dff = w_down.shape[0]
    dple = p2d.shape[1]
    nf = dff // tf
    halo = BF16_ROWS
    per = tm // halo
    last = m // halo - 1
    return pl.pallas_call(
        functools.partial(_ffn_kernel, alpha=alpha, tiles_per_seq=seq // tm),
        grid=(m // tm, nf),
        in_specs=[pl.BlockSpec((tm, d), lambda i, j: (i, 0)),
                  pl.BlockSpec((halo, d), lambda i, j: (jnp.maximum(i * per - 1, 0), 0)),
                  pl.BlockSpec((halo, d), lambda i, j: (jnp.minimum((i + 1) * per, last), 0)),
                  pl.BlockSpec((tm, d), lambda i, j: (i, 0)),
                  pl.BlockSpec((tm, dple), lambda i, j: (i, 0)),
                  pl.BlockSpec((d, 2 * tf), lambda i, j: (0, j)),
                  pl.BlockSpec((tf, d), lambda i, j: (j, 0)),
                  pl.BlockSpec((FFN_CONV_W, tf), lambda i, j: (0, j)),
                  pl.BlockSpec((1, tf), lambda i, j: (0, j)),
                  _const_spec((d, d)), _const_spec((dple, d)),
                  _const_spec((1, d)), _const_spec((1, d))],
        out_specs=[pl.BlockSpec((tm, d), lambda i, j: (i, 0)),
                   pl.BlockSpec((tm, d), lambda i, j: (i, 0))],
        out_shape=[jax.ShapeDtypeStruct((m, d), F32), jax.ShapeDtypeStruct((m, d), BF16)],
        scratch_shapes=[pltpu.VMEM((tm, d), F32)],
        compiler_params=_cparams("parallel", "arbitrary"),
        name="ffn_ple_ln",
    )(hb, hb, hb, h, p2d, w_up_tiled, w_down, cw, cb.reshape(1, dff), wpg, wple,
      g.reshape(1, d), b.reshape(1, d))


def _rope_tables(seq):
    half = ROPE_DIM // 2
    pos = jnp.arange(seq, dtype=F32)
    inv = ROPE_THETA ** (-jnp.arange(0, ROPE_DIM, 2, dtype=F32) / ROPE_DIM)
    ang = pos[:, None] * inv[None, :]
    cos, sin = jnp.cos(ang), jnp.sin(ang)
    zeros = jnp.zeros((seq, HEAD_DIM - ROPE_DIM), F32)
    zh = jnp.zeros((seq, half), F32)
    c = jnp.concatenate([cos, cos, zeros + 1.0], axis=1)
    s1 = jnp.concatenate([zh, sin, zeros], axis=1)
    s2 = jnp.concatenate([-sin, zh, zeros], axis=1)
    return c, s1, s2


def kernel(x, p, ln_in_g, ln_in_b, w_in, attn_sink, rnn_conv_w, rnn_conv_b, rg_w_a, rg_b_a, rg_w_x, rg_b_x, rg_lambda, w_proj_attn, w_proj_rnn, w_out, ln1_g, ln1_b, w_up, ffn_conv_w, ffn_conv_b, w_down, w_ple, w_ple_gate, ln2_g, ln2_b):
    nb, seq, d = x.shape
    depth = w_in.shape[0]
    dff = w_down.shape[1]
    m = nb * seq
    alpha = float((2 * depth) ** 0.25)
    n_qkv = (N_HEADS + 2 * N_KV_HEADS) * HEAD_DIM
    assert nb == SUBLANES and d == N_HEADS * HEAD_DIM and seq % WINDOW == 0 and seq >= 3 * WINDOW

    tm_proj = min(TM_PROJ, seq)
    tm_merge = min(TM_MERGE, seq)
    tm_ffn = min(TM_FFN, seq)
    ts_rnn = min(TS_RNN, seq)
    tn = TN
    nf = dff // tn

    rope_c, rope_s1, rope_s2 = _rope_tables(seq)
    h, hb = _ln_in(x.reshape(m, d), ln_in_g, ln_in_b, tm_proj)

    for l in range(depth):
        w_in_b = w_in[l].astype(BF16)
        qkv = _qkv_proj(hb, w_in_b[:, :n_qkv], rope_c, rope_s1, rope_s2, seq, tm_proj, tn)
        xr = _xr_proj(hb, w_in_b[:, n_qkv:n_qkv + d], tm_proj, tn)

        wg = jnp.concatenate([rg_w_a[l], rg_w_x[l]], axis=-1).astype(BF16)
        hf, hbk = _rnn(xr.reshape(nb, seq, d), rnn_conv_w[l], rnn_conv_b[l], wg,
                       rg_b_a[l], rg_b_x[l], rg_lambda[l], ts_rnn)
        attn = _attention(qkv.reshape(nb, seq, n_qkv), attn_sink[l])

        h, hb = _merge(hb, attn.reshape(m, d), hf.reshape(m, d), hbk.reshape(m, d), h,
                       w_in_b[:, n_qkv + d:], w_proj_attn[l].astype(BF16),
                       w_proj_rnn[l].astype(BF16), w_out[l].astype(BF16),
                       ln1_g[l], ln1_b[l], alpha, tm_merge)

        w_up_tiled = (w_up[l].astype(BF16).reshape(d, 2, dff // MXU_COLS, MXU_COLS)
                      .transpose(0, 2, 1, 3).reshape(d, 2 * dff))
        h, hb = _ffn(h, hb, p[l].reshape(m, -1), w_up_tiled, w_down[l].astype(BF16),
                     ffn_conv_w[l], ffn_conv_b[l], w_ple_gate[l].astype(BF16),
                     w_ple[l].astype(BF16), ln2_g[l], ln2_b[l], alpha, seq, tm_ffn, tn)
    return h.reshape(nb, seq, d)
```

```python
import functools
import math

import jax
import jax.numpy as jnp
from jax import lax
from jax.experimental import pallas as pl
from jax.experimental.pallas import tpu as pltpu

F32 = jnp.float32
BF16 = jnp.bfloat16

N_HEADS = 8
N_KV_HEADS = 2
HEAD_DIM = 128
Q_GROUP = N_HEADS // N_KV_HEADS
WINDOW = 128
ROPE_THETA = 500000.0
ROPE_DIM = HEAD_DIM // 4
N_RNN_BLOCKS = 4
RNN_CONV_W = 4
FFN_CONV_W = 3
RG_C = 8.0
LN_EPS = 1e-5
NEG_INF = -1e30
LOG2E = math.log2(math.e)
GELU_C = math.sqrt(2.0 / math.pi)
GELU_A = 0.044715

LANES = 128
SUBLANES = 8
BF16_ROWS = 16
MXU_COLS = 256
VMEM_LIMIT = 56 * 1024 * 1024

TM_LN = 1024
TM_PROJ = 2048
TM_MERGE = 512
TM_FFN = 512
TS_RNN = 256
TN = 512


def _cparams(*sem):
    return pltpu.CompilerParams(dimension_semantics=sem, vmem_limit_bytes=VMEM_LIMIT)


def _layer_norm(x, g, b):
    mu = jnp.mean(x, axis=-1, keepdims=True)
    xc = x - mu
    var = jnp.mean(xc * xc, axis=-1, keepdims=True)
    return xc * lax.rsqrt(var + LN_EPS) * g + b


def _gelu(x):
    k1 = -2.0 * GELU_C * LOG2E
    return x / (1.0 + jnp.exp2(x * (k1 + (k1 * GELU_A) * (x * x))))


def _softplus(x):
    return jnp.maximum(x, 0.0) + jnp.log1p(jnp.exp(-jnp.abs(x)))


def _const_spec(shape):
    nd = len(shape)
    return pl.BlockSpec(shape, lambda *_: (0,) * nd, pipeline_mode=pl.Buffered(1))


def _slab_scratch(rows, cols):
    return pltpu.VMEM((cols // LANES, rows, LANES), F32)


def _put_sequence(slab_s, b, val):
    tt, n = val.shape
    for c in range(n // LANES):
        slab_s[c, pl.ds(b, tt, stride=SUBLANES), :] = val[:, c * LANES:(c + 1) * LANES]


def _get_sequence(slab_s, b, tt, c):
    return slab_s[c, pl.ds(b, tt, stride=SUBLANES), :]


def _slabs(slab_s):
    return jnp.concatenate([slab_s[c] for c in range(slab_s.shape[0])], axis=1)


def _ln_in_kernel(x_ref, g_ref, b_ref, h_ref, hb_ref, t_s):
    nb = x_ref.shape[0]
    for b in range(nb):
        _put_sequence(t_s, b, _layer_norm(x_ref[b], g_ref[...], b_ref[...]))
    y = _slabs(t_s)
    h_ref[...] = y
    hb_ref[...] = y.astype(BF16)


def _ln_in(x, g, b, tm):
    nb, seq, d = x.shape
    m = nb * seq
    tt = tm // nb
    return pl.pallas_call(
        _ln_in_kernel,
        grid=(m // tm,),
        in_specs=[pl.BlockSpec((nb, tt, d), lambda i: (0, i, 0)),
                  _const_spec((1, d)), _const_spec((1, d))],
        out_specs=[pl.BlockSpec((tm, d), lambda i: (i, 0)),
                   pl.BlockSpec((tm, d), lambda i: (i, 0))],
        out_shape=[jax.ShapeDtypeStruct((m, d), F32), jax.ShapeDtypeStruct((m, d), BF16)],
        scratch_shapes=[_slab_scratch(tm, d)],
        compiler_params=_cparams("parallel"),
        name="ln_in",
    )(x, g.reshape(1, d), b.reshape(1, d))


def _qkv_kernel(h_ref, w_ref, c_ref, s_ref, o_ref, z_s, *, n_rope_last):
    j = pl.program_id(1)
    last = pl.num_programs(1) - 1
    nb, tt, _ = o_ref.shape
    c, s = c_ref[...], s_ref[...]
    h = h_ref[...]

    def rope(zc):
        return zc * c + pltpu.roll(zc, LANES // 2, 1) * s

    heads_per_dot = MXU_COLS // LANES
    for cd in range(w_ref.shape[1] // MXU_COLS):
        z = jnp.dot(h, w_ref[:, cd * MXU_COLS:(cd + 1) * MXU_COLS], preferred_element_type=F32)
        for ch in range(heads_per_dot):
            cc = cd * heads_per_dot + ch
            z_s[cc] = z[:, ch * LANES:(ch + 1) * LANES]
    def emit(cc, rotary):
        for b in range(nb):
            zc = _get_sequence(z_s, b, tt, cc)
            o_ref[b, :, cc * LANES:(cc + 1) * LANES] = (rope(zc) if rotary else zc).astype(BF16)

    for cc in range(z_s.shape[0]):
        if cc < n_rope_last:
            emit(cc, True)
        else:
            pl.when(j < last)(functools.partial(emit, cc, True))
            pl.when(j == last)(functools.partial(emit, cc, False))


def _qkv_proj(hb, w, rope_c, rope_s, nb, tm, tn):
    m, d = hb.shape
    n = w.shape[1]
    seq = m // nb
    tt = tm // nb
    n_rope_cols = (N_HEADS + N_KV_HEADS) * HEAD_DIM
    assert n % tn == 0 and tn % MXU_COLS == 0 and (n - tn) <= n_rope_cols <= n
    n_rope_last = (n_rope_cols - (n - tn)) // LANES
    tab = pl.BlockSpec((tt, LANES), lambda i, j: (i, 0))
    return pl.pallas_call(
        functools.partial(_qkv_kernel, n_rope_last=n_rope_last),
        grid=(m // tm, n // tn),
        in_specs=[pl.BlockSpec((tm, d), lambda i, j: (i, 0)),
                  pl.BlockSpec((d, tn), lambda i, j: (0, j)),
                  tab, tab],
        out_specs=pl.BlockSpec((nb, tt, tn), lambda i, j: (0, i, j)),
        out_shape=jax.ShapeDtypeStruct((nb, seq, n), BF16),
        scratch_shapes=[_slab_scratch(tm, tn)],
        compiler_params=_cparams("parallel", "arbitrary"),
        name="qkv_proj",
    )(hb, w, rope_c, rope_s)


def _proj_kernel(h_ref, w_ref, o_ref):
    o_ref[...] = jnp.dot(h_ref[...], w_ref[...], preferred_element_type=F32)


def _xr_proj(hb, w, tm, tn):
    m, d = hb.shape
    n = w.shape[1]
    return pl.pallas_call(
        _proj_kernel,
        grid=(m // tm, n // tn),
        in_specs=[pl.BlockSpec((tm, d), lambda i, j: (i, 0)),
                  pl.BlockSpec((d, tn), lambda i, j: (0, j))],
        out_specs=pl.BlockSpec((tm, tn), lambda i, j: (i, j)),
        out_shape=jax.ShapeDtypeStruct((m, n), F32),
        compiler_params=_cparams("parallel", "arbitrary"),
        name="xr_proj",
    )(hb, w)


def _rnn_kernel(xf_ref, xb_ref, cw_ref, cb_ref, wg_ref, ba_ref, bx_ref, lam_ref,
                hf_ref, hb_ref, xsf_s, xsb_s, af_s, uf_s, ab_s, ub_s, carry_s, *, nb):
    i = pl.program_id(1)
    rows, bw = xf_ref.shape
    ts = rows // nb
    halo = (RNN_CONV_W - 1) * nb

    @pl.when(i == 0)
    def _():
        xsf_s[0:halo, :] = jnp.zeros((halo, bw), F32)
        xsb_s[rows:rows + halo, :] = jnp.zeros((halo, bw), F32)
        carry_s[...] = jnp.zeros_like(carry_s)

    row = lax.broadcasted_iota(jnp.int32, (rows, bw), 0)

    def gates(d, xc, start_mask, a_s, u_s):
        g = jnp.dot(xc.astype(BF16), wg_ref[d, 0], preferred_element_type=F32)
        gate_a = 1.0 / (1.0 + jnp.exp2(g[:, :bw] * (-LOG2E) + ba_ref[d] * (-LOG2E)))
        gate_x = 1.0 / (1.0 + jnp.exp2(g[:, bw:] * (-LOG2E) + bx_ref[d] * (-LOG2E)))
        log_a = gate_a * ((-RG_C) * _softplus(-lam_ref[d]))
        a = jnp.exp(log_a)
        w = jnp.tanh(log_a) * (-1.0 - a * a)
        mult = jnp.where(w > 0.0, w * lax.rsqrt(w), 0.0)
        mult = jnp.where(start_mask, 1.0, mult)
        a_s[...] = a
        u_s[...] = xc * gate_x * mult

    xsf_s[halo:halo + rows, :] = xf_ref[...]
    yf = cb_ref[0] + cw_ref[0, 0] * xsf_s[halo:halo + rows, :]
    for k in range(1, RNN_CONV_W):
        yf = yf + cw_ref[0, k] * xsf_s[halo - k * nb:halo - k * nb + rows, :]
    xsf_s[0:halo, :] = xsf_s[rows:rows + halo, :]
    gates(0, yf, (row < nb) & (i == 0), af_s, uf_s)

    xsb_s[0:rows, :] = xb_ref[...]
    yb = cb_ref[1] + cw_ref[1, 0] * xsb_s[0:rows, :]
    for k in range(1, RNN_CONV_W):
        yb = yb + cw_ref[1, k] * xsb_s[k * nb:k * nb + rows, :]
    xsb_s[rows:rows + halo, :] = xsb_s[0:halo, :]
    gates(1, yb, (row >= rows - nb) & (i == 0), ab_s, ub_s)

    def step(tt, carry):
        cf, cb = carry
        rf = pl.ds(pl.multiple_of(tt * nb, nb), nb)
        rb = pl.ds(pl.multiple_of((ts - 1 - tt) * nb, nb), nb)
        cf = af_s[rf, :] * cf + uf_s[rf, :]
        cb = ab_s[rb, :] * cb + ub_s[rb, :]
        hf_ref[rf, :] = cf
        hb_ref[rb, :] = cb
        return cf, cb

    cf, cb = lax.fori_loop(0, ts, step, (carry_s[0], carry_s[1]), unroll=8)
    carry_s[0] = cf
    carry_s[1] = cb


def _rnn(xr, conv_w, conv_b, wg, b_a, b_x, lam, nb, ts):
    m, d = xr.shape
    bw = d // N_RNN_BLOCKS
    rows = ts * nb
    nt = m // rows
    halo = (RNN_CONV_W - 1) * nb
    xs = pltpu.VMEM((rows + halo, bw), F32)
    au = pltpu.VMEM((rows, bw), F32)
    vec = lambda: pl.BlockSpec((2, 1, bw), lambda c, i: (0, 0, c))
    return pl.pallas_call(
        functools.partial(_rnn_kernel, nb=nb),
        grid=(N_RNN_BLOCKS, nt),
        in_specs=[pl.BlockSpec((rows, bw), lambda c, i: (i, c)),
                  pl.BlockSpec((rows, bw), lambda c, i: (nt - 1 - i, c)),
                  pl.BlockSpec((2, RNN_CONV_W, 1, bw), lambda c, i: (0, 0, 0, c)),
                  vec(),
                  pl.BlockSpec((2, 1, bw, 2 * bw), lambda c, i: (0, c, 0, 0)),
                  vec(), vec(), vec()],
        out_specs=[pl.BlockSpec((rows, bw), lambda c, i: (i, c)),
                   pl.BlockSpec((rows, bw), lambda c, i: (nt - 1 - i, c))],
        out_shape=[jax.ShapeDtypeStruct((m, d), F32)] * 2,
        scratch_shapes=[xs, xs, au, au, au, au, pltpu.VMEM((2, nb, bw), F32)],
        compiler_params=_cparams("parallel", "arbitrary"),
        name="rg_lru",
    )(xr, xr, conv_w.reshape(2, RNN_CONV_W, 1, d), conv_b.reshape(2, 1, d), wg,
      b_a.reshape(2, 1, d), b_x.reshape(2, 1, d), lam.reshape(2, 1, d))


def _attn_kernel(sink_ref, q_ref, k_ref, v_ref, o_ref, kp_s, vp_s, s_s, p_s):
    hk = pl.program_id(1)
    seq = q_ref.shape[1]
    blk = WINDOW
    nblk = seq // blk
    span = 3 * blk
    c1 = HEAD_DIM ** -0.5 * LOG2E

    kp_s[0:blk, :] = jnp.zeros((blk, HEAD_DIM), BF16)
    kp_s[blk:blk + seq, :] = k_ref[0]
    kp_s[blk + seq:2 * blk + seq, :] = jnp.zeros((blk, HEAD_DIM), BF16)
    vp_s[0:blk, 0:HEAD_DIM] = jnp.zeros((blk, HEAD_DIM), BF16)
    vp_s[blk:blk + seq, 0:HEAD_DIM] = v_ref[0]
    vp_s[blk + seq:2 * blk + seq, 0:HEAD_DIM] = jnp.zeros((blk, HEAD_DIM), BF16)
    vp_s[:, HEAD_DIM:] = jnp.ones((seq + 2 * blk, HEAD_DIM), BF16)

    qi = lax.broadcasted_iota(jnp.int32, (blk, blk), 0)
    kj = lax.broadcasted_iota(jnp.int32, (blk, blk), 1)
    bias_prev = jnp.where(kj >= qi, 0.0, NEG_INF)
    bias_next = jnp.where(kj <= qi, 0.0, NEG_INF)

    def logits(n, slot):
        q0 = pl.multiple_of(n * blk, blk)
        qs = jnp.concatenate(
            [q_ref[0, pl.ds(q0, blk), g * HEAD_DIM:(g + 1) * HEAD_DIM] for g in range(Q_GROUP)], axis=0)
        s_s[slot] = lax.dot_general(qs, kp_s[pl.ds(q0, span), :], (((1,), (1,)), ((), ())),
                                    preferred_element_type=F32)

    logits(0, 0)

    def block(n, slot):
        logits(jnp.minimum(n + 1, nblk - 1), 1 - slot)
        q0 = pl.multiple_of(n * blk, blk)
        bp = jnp.where(n == 0, NEG_INF, bias_prev)
        bn = jnp.where(n == nblk - 1, NEG_INF, bias_next)
        m2s = []
        for g in range(Q_GROUP):
            r = slice(g * blk, (g + 1) * blk)
            s_prev = s_s[slot, r, 0:blk] + bp
            s_cur = s_s[slot, r, blk:2 * blk]
            s_next = s_s[slot, r, 2 * blk:span] + bn
            m_raw = jnp.max(jnp.maximum(jnp.maximum(s_prev, s_cur), s_next), axis=-1, keepdims=True)
            m2 = jnp.maximum(m_raw * c1, sink_ref[hk * Q_GROUP + g] * LOG2E)
            p_s[slot, r, 0:blk] = jnp.exp2(s_prev * c1 - m2).astype(BF16)
            p_s[slot, r, blk:2 * blk] = jnp.exp2(s_cur * c1 - m2).astype(BF16)
            p_s[slot, r, 2 * blk:span] = jnp.exp2(s_next * c1 - m2).astype(BF16)
            m2s.append(m2)
        oe = jnp.dot(p_s[slot], vp_s[pl.ds(q0, span), :], preferred_element_type=F32)
        for g in range(Q_GROUP):
            r = slice(g * blk, (g + 1) * blk)
            den = oe[r, HEAD_DIM:] + jnp.exp2(sink_ref[hk * Q_GROUP + g] * LOG2E - m2s[g])
            o_ref[0, pl.ds(q0, blk), g * HEAD_DIM:(g + 1) * HEAD_DIM] = (oe[r, :HEAD_DIM] / den).astype(BF16)

    def body(i, carry):
        block(2 * i, 0)
        block(2 * i + 1, 1)
        return carry

    assert nblk % 2 == 0
    lax.fori_loop(0, nblk // 2, body, 0)


def _attention(qkv3, sink):
    nb, seq, _ = qkv3.shape
    gw = Q_GROUP * HEAD_DIM
    k_blk0 = N_HEADS
    v_blk0 = N_HEADS + N_KV_HEADS
    rows = Q_GROUP * WINDOW
    return pl.pallas_call(
        _attn_kernel,
        grid=(nb, N_KV_HEADS),
        in_specs=[pl.BlockSpec(memory_space=pltpu.SMEM),
                  pl.BlockSpec((1, seq, gw), lambda b, h: (b, 0, h)),
                  pl.BlockSpec((1, seq, HEAD_DIM), lambda b, h: (b, 0, k_blk0 + h)),
                  pl.BlockSpec((1, seq, HEAD_DIM), lambda b, h: (b, 0, v_blk0 + h))],
        out_specs=pl.BlockSpec((1, seq, gw), lambda b, h: (b, 0, h)),
        out_shape=jax.ShapeDtypeStruct((nb, seq, N_HEADS * HEAD_DIM), BF16),
        scratch_shapes=[pltpu.VMEM((seq + 2 * WINDOW, HEAD_DIM), BF16),
                        pltpu.VMEM((seq + 2 * WINDOW, 2 * HEAD_DIM), BF16),
                        pltpu.VMEM((2, rows, 3 * WINDOW), F32),
                        pltpu.VMEM((2, rows, 3 * WINDOW), BF16)],
        compiler_params=_cparams("parallel", "parallel"),
        name="swa_attention",
    )(sink, qkv3, qkv3, qkv3)


def _merge_kernel(hb_ref, attn_ref, hf_ref, hbk_ref, h_ref,
                  wz_ref, wpa_ref, wpr_ref, wo_ref, g_ref, b_ref, o_ref, ob_ref, at_s, *, alpha):
    d = h_ref.shape[1]
    nb = attn_ref.shape[0]
    z = jnp.dot(hb_ref[...], wz_ref[...], preferred_element_type=F32)
    rnn = (hf_ref[...] + hbk_ref[...]) * _gelu(z[:, :d])
    for b in range(nb):
        _put_sequence(at_s, b, attn_ref[b].astype(F32))
    pa = jnp.dot(_slabs(at_s).astype(BF16), wpa_ref[...], preferred_element_type=F32)
    pr = jnp.dot(rnn.astype(BF16), wpr_ref[...], preferred_element_type=F32)
    merged = jax.nn.sigmoid(z[:, d:2 * d]) * pa + jax.nn.sigmoid(z[:, 2 * d:]) * pr
    o = jnp.dot(merged.astype(BF16), wo_ref[...], preferred_element_type=F32)
    y = _layer_norm(alpha * h_ref[...] + o, g_ref[...], b_ref[...])
    o_ref[...] = y
    ob_ref[...] = y.astype(BF16)


def _merge(hb, attn3, hf, hbk, h, wz, wpa, wpr, wo, g, b, alpha, tm):
    m, d = h.shape
    nb = attn3.shape[0]
    row = pl.BlockSpec((tm, d), lambda i: (i, 0))
    return pl.pallas_call(
        functools.partial(_merge_kernel, alpha=alpha),
        grid=(m // tm,),
        in_specs=[row, pl.BlockSpec((nb, tm // nb, d), lambda i: (0, i, 0)), row, row, row,
                  _const_spec(wz.shape), _const_spec((d, d)), _const_spec((d, d)), _const_spec((d, d)),
                  _const_spec((1, d)), _const_spec((1, d))],
        out_specs=[row, row],
        out_shape=[jax.ShapeDtypeStruct((m, d), F32), jax.ShapeDtypeStruct((m, d), BF16)],
        scratch_shapes=[_slab_scratch(tm, d)],
        compiler_params=_cparams("parallel"),
        name="merge_out_ln",
    )(hb, attn3, hf, hbk, h, wz, wpa, wpr, wo, g.reshape(1, d), b.reshape(1, d))


def _ffn_kernel(hb_ref, hprev_ref, hnext_ref, h_ref, p_ref, wu_ref, wd_ref, cw_ref, cb_ref,
                wpg_ref, wple_ref, g_ref, b_ref, *rest, alpha, tf, nb, batch_major_out):
    if batch_major_out:
        o_ref, p_s, y_s = rest
    else:
        o_ref, ob_ref, p_s = rest
    i = pl.program_id(0)
    tm = hb_ref.shape[0]
    dff = wd_ref.shape[0]
    halo = hprev_ref.shape[0]
    hb = hb_ref[...]
    hp = hprev_ref[...]
    hn = hnext_ref[...]
    hp = jnp.where(i == 0, jnp.zeros_like(hp), hp)
    hn = jnp.where(i == pl.num_programs(0) - 1, jnp.zeros_like(hn), hn)
    lhs = jnp.concatenate([hp, hb, hn], axis=0)
    acc = None
    for j in range(dff // tf):
        cs = slice(j * tf, (j + 1) * tf)
        gate = jnp.dot(lhs, wu_ref[:, cs], preferred_element_type=F32)
        val = jnp.dot(hb, wu_ref[:, dff + j * tf:dff + (j + 1) * tf], preferred_element_type=F32)
        conv = (cw_ref[0:1, cs] * gate[halo - nb:halo - nb + tm] + cw_ref[1:2, cs] * gate[halo:halo + tm]
                + cw_ref[2:3, cs] * gate[halo + nb:halo + nb + tm] + cb_ref[:, cs])
        act = (_gelu(conv) * val).astype(BF16)
        pc = jnp.dot(act, wd_ref[cs, :], preferred_element_type=F32)
        acc = pc if acc is None else acc + pc

    for b in range(nb):
        _put_sequence(p_s, b, p_ref[b])
    ple = (jax.nn.sigmoid(jnp.dot(hb, wpg_ref[...], preferred_element_type=F32))
           * jnp.dot(_slabs(p_s).astype(BF16), wple_ref[...], preferred_element_type=F32))
    y = _layer_norm(alpha * h_ref[...] + acc + ple, g_ref[...], b_ref[...])
    if batch_major_out:
        for c in range(y_s.shape[0]):
            y_s[c] = y[:, c * LANES:(c + 1) * LANES]
        for b in range(nb):
            for c in range(y_s.shape[0]):
                o_ref[b, :, c * LANES:(c + 1) * LANES] = _get_sequence(y_s, b, tm // nb, c)
    else:
        o_ref[...] = y
        ob_ref[...] = y.astype(BF16)


def _ffn(h, hb, p3, w_up, w_down, cw, cb, wpg, wple, g, b, alpha, tm, tf, batch_major_out):
    m, d = h.shape
    dff = w_down.shape[0]
    nb, seq, dple = p3.shape
    tt = tm // nb
    halo = BF16_ROWS
    per = tm // halo
    last = m // halo - 1
    row = pl.BlockSpec((tm, d), lambda i: (i, 0))
    if batch_major_out:
        out_specs = pl.BlockSpec((nb, tt, d), lambda i: (0, i, 0))
        out_shape = jax.ShapeDtypeStruct((nb, seq, d), F32)
        scratch = [_slab_scratch(tm, dple), _slab_scratch(tm, d)]
    else:
        out_specs = [row, row]
        out_shape = [jax.ShapeDtypeStruct((m, d), F32), jax.ShapeDtypeStruct((m, d), BF16)]
        scratch = [_slab_scratch(tm, dple)]
    return pl.pallas_call(
        functools.partial(_ffn_kernel, alpha=alpha, tf=tf, nb=nb, batch_major_out=batch_major_out),
        grid=(m // tm,),
        in_specs=[row,
                  pl.BlockSpec((halo, d), lambda i: (jnp.maximum(i * per - 1, 0), 0)),
                  pl.BlockSpec((halo, d), lambda i: (jnp.minimum((i + 1) * per, last), 0)),
                  row,
                  pl.BlockSpec((nb, tt, dple), lambda i: (0, i, 0)),
                  _const_spec((d, 2 * dff)), _const_spec((dff, d)),
                  _const_spec((FFN_CONV_W, dff)), _const_spec((1, dff)),
                  _const_spec((d, d)), _const_spec((dple, d)),
                  _const_spec((1, d)), _const_spec((1, d))],
        out_specs=out_specs,
        out_shape=out_shape,
        scratch_shapes=scratch,
        compiler_params=_cparams("parallel"),
        name="ffn_ple_ln",
    )(hb, hb, hb, h, p3, w_up, w_down, cw, cb.reshape(1, dff), wpg, wple,
      g.reshape(1, d), b.reshape(1, d))


def _rope_head_perm():
    half = ROPE_DIM // 2
    mid = HEAD_DIM // 2
    rest = list(range(ROPE_DIM, HEAD_DIM))
    return (list(range(half)) + rest[:mid - half] + list(range(half, ROPE_DIM)) + rest[mid - half:])


def _rope_tables(seq):
    half = ROPE_DIM // 2
    mid = HEAD_DIM // 2
    pos = jnp.arange(seq, dtype=F32)
    inv = ROPE_THETA ** (-jnp.arange(0, ROPE_DIM, 2, dtype=F32) / ROPE_DIM)
    ang = pos[:, None] * inv[None, :]
    cos, sin = jnp.cos(ang), jnp.sin(ang)
    ones = jnp.ones((seq, mid - half), F32)
    zeros = jnp.zeros((seq, mid - half), F32)
    c = jnp.concatenate([cos, ones, cos, ones], axis=1)
    s = jnp.concatenate([-sin, zeros, sin, zeros], axis=1)
    return c, s


def kernel(x, p, ln_in_g, ln_in_b, w_in, attn_sink, rnn_conv_w, rnn_conv_b, rg_w_a, rg_b_a, rg_w_x, rg_b_x, rg_lambda, w_proj_attn, w_proj_rnn, w_out, ln1_g, ln1_b, w_up, ffn_conv_w, ffn_conv_b, w_down, w_ple, w_ple_gate, ln2_g, ln2_b):
    nb, seq, d = x.shape
    depth = w_in.shape[0]
    m = nb * seq
    alpha = float((2 * depth) ** 0.25)
    n_qkv = (N_HEADS + 2 * N_KV_HEADS) * HEAD_DIM
    assert nb == SUBLANES and d == N_HEADS * HEAD_DIM and seq % WINDOW == 0 and seq >= 3 * WINDOW

    tm_ln = min(TM_LN, m)
    tm_proj = min(TM_PROJ, m)
    tm_merge = min(TM_MERGE, m)
    tm_ffn = min(TM_FFN, m)
    ts_rnn = min(TS_RNN, seq)

    head_perm = _rope_head_perm()
    qkv_cols = [hd * HEAD_DIM + c for hd in range(N_HEADS + N_KV_HEADS) for c in head_perm]
    qkv_cols += list(range(len(qkv_cols), n_qkv))
    w_qkv = jnp.take(w_in, jnp.asarray(qkv_cols, jnp.int32), axis=2).astype(BF16)
    w_xr = w_in[:, :, n_qkv:n_qkv + d].astype(BF16)
    w_z = w_in[:, :, n_qkv + d:].astype(BF16)
    w_g = jnp.concatenate([rg_w_a, rg_w_x], axis=-1).astype(BF16)
    w_pa, w_pr, w_o = w_proj_attn.astype(BF16), w_proj_rnn.astype(BF16), w_out.astype(BF16)
    w_u, w_d = w_up.astype(BF16), w_down.astype(BF16)
    w_pg, w_pl = w_ple_gate.astype(BF16), w_ple.astype(BF16)

    rope_c, rope_s = _rope_tables(seq)
    h, hb = _ln_in(x, ln_in_g, ln_in_b, tm_ln)

    for l in range(depth):
        qkv = _qkv_proj(hb, w_qkv[l], rope_c, rope_s, nb, tm_proj, TN)
        xr = _xr_proj(hb, w_xr[l], tm_proj, TN)
        hf, hbk = _rnn(xr, rnn_conv_w[l], rnn_conv_b[l], w_g[l], rg_b_a[l], rg_b_x[l], rg_lambda[l],
                       nb, ts_rnn)
        attn = _attention(qkv, attn_sink[l])
        h, hb = _merge(hb, attn, hf, hbk, h, w_z[l], w_pa[l], w_pr[l], w_o[l],
                       ln1_g[l], ln1_b[l], alpha, tm_merge)
        out = _ffn(h, hb, p[l], w_u[l], w_d[l], ffn_conv_w[l], ffn_conv_b[l], w_pg[l], w_pl[l],
                   ln2_g[l], ln2_b[l], alpha, tm_ffn, TN, batch_major_out=(l == depth - 1))
        if l < depth - 1:
            h, hb = out
    return out
```

```python
import functools
import math

import jax
import jax.numpy as jnp
from jax import lax
from jax.experimental import pallas as pl
from jax.experimental.pallas import tpu as pltpu

F32 = jnp.float32
BF16 = jnp.bfloat16

N_HEADS = 8
N_KV_HEADS = 2
HEAD_DIM = 128
Q_GROUP = N_HEADS // N_KV_HEADS
WINDOW = 128
ROPE_THETA = 500000.0
ROPE_DIM = HEAD_DIM // 4
N_RNN_BLOCKS = 4
RNN_CONV_W = 4
FFN_CONV_W = 3
RG_C = 8.0
LN_EPS = 1e-5
NEG_INF = -1e30
LOG2E = math.log2(math.e)
GELU_C = math.sqrt(2.0 / math.pi)
GELU_A = 0.044715

LANES = 128
SUBLANES = 8
BF16_ROWS = 16
MXU_COLS = 256
VMEM_LIMIT = 56 * 1024 * 1024

TM_LN = 1024
TM_PROJ = 2048
TM_MERGE = 512
TM_FFN = 512
TS_RNN = 256
TN = 512


def _cparams(*sem):
    return pltpu.CompilerParams(dimension_semantics=sem, vmem_limit_bytes=VMEM_LIMIT)


def _layer_norm(x, g, b):
    mu = jnp.mean(x, axis=-1, keepdims=True)
    xc = x - mu
    var = jnp.mean(xc * xc, axis=-1, keepdims=True)
    return xc * lax.rsqrt(var + LN_EPS) * g + b


def _gelu(x):
    k1 = -2.0 * GELU_C * LOG2E
    return x / (1.0 + jnp.exp2(x * (k1 + (k1 * GELU_A) * (x * x))))


def _softplus(x):
    return jnp.maximum(x, 0.0) + jnp.log1p(jnp.exp(-jnp.abs(x)))


def _const_spec(shape):
    nd = len(shape)
    return pl.BlockSpec(shape, lambda *_: (0,) * nd, pipeline_mode=pl.Buffered(1))


def _layer_spec(arr, l):
    tail = (0,) * (arr.ndim - 1)
    return pl.BlockSpec((None,) + arr.shape[1:], lambda *_: (l,) + tail, pipeline_mode=pl.Buffered(1))


def _slab_scratch(rows, cols):
    return pltpu.VMEM((cols // LANES, rows, LANES), F32)


def _put_sequence(slab_s, b, val):
    tt, n = val.shape
    for c in range(n // LANES):
        slab_s[c, pl.ds(b, tt, stride=SUBLANES), :] = val[:, c * LANES:(c + 1) * LANES]


def _get_sequence(slab_s, b, tt, c):
    return slab_s[c, pl.ds(b, tt, stride=SUBLANES), :]


def _slabs(slab_s):
    return jnp.concatenate([slab_s[c] for c in range(slab_s.shape[0])], axis=1)


def _ln_in_kernel(x_ref, g_ref, b_ref, h_ref, hb_ref, t_s):
    nb = x_ref.shape[0]
    for b in range(nb):
        _put_sequence(t_s, b, _layer_norm(x_ref[b], g_ref[...], b_ref[...]))
    y = _slabs(t_s)
    h_ref[...] = y
    hb_ref[...] = y.astype(BF16)


def _ln_in(x, g, b, tm):
    nb, seq, d = x.shape
    m = nb * seq
    tt = tm // nb
    return pl.pallas_call(
        _ln_in_kernel,
        grid=(m // tm,),
        in_specs=[pl.BlockSpec((nb, tt, d), lambda i: (0, i, 0)),
                  _const_spec((1, d)), _const_spec((1, d))],
        out_specs=[pl.BlockSpec((tm, d), lambda i: (i, 0)),
                   pl.BlockSpec((tm, d), lambda i: (i, 0))],
        out_shape=[jax.ShapeDtypeStruct((m, d), F32), jax.ShapeDtypeStruct((m, d), BF16)],
        scratch_shapes=[_slab_scratch(tm, d)],
        compiler_params=_cparams("parallel"),
        name="ln_in",
    )(x, g.reshape(1, d), b.reshape(1, d))


def _qkv_kernel(h_ref, w_ref, c_ref, s_ref, o_ref, z_s, *, n_rope_last):
    j = pl.program_id(1)
    last = pl.num_programs(1) - 1
    nb, tt, _ = o_ref.shape
    c, s = c_ref[...], s_ref[...]
    h = h_ref[...]

    def rope(zc):
        return zc * c + pltpu.roll(zc, LANES // 2, 1) * s

    heads_per_dot = MXU_COLS // LANES
    for cd in range(w_ref.shape[1] // MXU_COLS):
        z = jnp.dot(h, w_ref[:, cd * MXU_COLS:(cd + 1) * MXU_COLS], preferred_element_type=F32)
        for ch in range(heads_per_dot):
            cc = cd * heads_per_dot + ch
            z_s[cc] = z[:, ch * LANES:(ch + 1) * LANES]
    def emit(cc, rotary):
        for b in range(nb):
            zc = _get_sequence(z_s, b, tt, cc)
            o_ref[b, :, cc * LANES:(cc + 1) * LANES] = (rope(zc) if rotary else zc).astype(BF16)

    for cc in range(z_s.shape[0]):
        if cc < n_rope_last:
            emit(cc, True)
        else:
            pl.when(j < last)(functools.partial(emit, cc, True))
            pl.when(j == last)(functools.partial(emit, cc, False))


def _qkv_proj(hb, l, w, rope_c, rope_s, nb, tm, tn):
    m, d = hb.shape
    n = w.shape[2]
    seq = m // nb
    tt = tm // nb
    n_rope_cols = (N_HEADS + N_KV_HEADS) * HEAD_DIM
    assert n % tn == 0 and tn % MXU_COLS == 0 and (n - tn) <= n_rope_cols <= n
    n_rope_last = (n_rope_cols - (n - tn)) // LANES
    tab = pl.BlockSpec((tt, LANES), lambda i, j: (i, 0))
    return pl.pallas_call(
        functools.partial(_qkv_kernel, n_rope_last=n_rope_last),
        grid=(m // tm, n // tn),
        in_specs=[pl.BlockSpec((tm, d), lambda i, j: (i, 0)),
                  pl.BlockSpec((None, d, tn), lambda i, j: (l, 0, j)),
                  tab, tab],
        out_specs=pl.BlockSpec((nb, tt, tn), lambda i, j: (0, i, j)),
        out_shape=jax.ShapeDtypeStruct((nb, seq, n), BF16),
        scratch_shapes=[_slab_scratch(tm, tn)],
        compiler_params=_cparams("parallel", "arbitrary"),
        name="qkv_proj",
    )(hb, w, rope_c, rope_s)


def _proj_kernel(h_ref, w_ref, o_ref):
    o_ref[...] = jnp.dot(h_ref[...], w_ref[...], preferred_element_type=F32)


def _xr_proj(hb, l, w, tm, tn):
    m, d = hb.shape
    n = w.shape[2]
    return pl.pallas_call(
        _proj_kernel,
        grid=(m // tm, n // tn),
        in_specs=[pl.BlockSpec((tm, d), lambda i, j: (i, 0)),
                  pl.BlockSpec((None, d, tn), lambda i, j: (l, 0, j))],
        out_specs=pl.BlockSpec((tm, tn), lambda i, j: (i, j)),
        out_shape=jax.ShapeDtypeStruct((m, n), F32),
        compiler_params=_cparams("parallel", "arbitrary"),
        name="xr_proj",
    )(hb, w)


def _rnn_kernel(xf_ref, xb_ref, cw_ref, cb_ref, wg_ref, ba_ref, bx_ref, lam_ref,
                hf_ref, hb_ref, xsf_s, xsb_s, af_s, uf_s, ab_s, ub_s, carry_s, *, nb):
    i = pl.program_id(1)
    rows, bw = hf_ref.shape
    ts = rows // nb
    halo = (RNN_CONV_W - 1) * nb

    @pl.when(i == 0)
    def _():
        xsf_s[0:halo, :] = jnp.zeros((halo, bw), F32)
        xsb_s[rows:rows + halo, :] = jnp.zeros((halo, bw), F32)
        carry_s[...] = jnp.zeros_like(carry_s)

    row = lax.broadcasted_iota(jnp.int32, (rows, bw), 0)

    def gates(d, xc, start_mask, a_s, u_s):
        g = jnp.dot(xc.astype(BF16), wg_ref[d, 0], preferred_element_type=F32)
        gate_a = 1.0 / (1.0 + jnp.exp2(g[:, :bw] * (-LOG2E) + ba_ref[d] * (-LOG2E)))
        gate_x = 1.0 / (1.0 + jnp.exp2(g[:, bw:] * (-LOG2E) + bx_ref[d] * (-LOG2E)))
        log_a = gate_a * ((-RG_C) * _softplus(-lam_ref[d]))
        a = jnp.exp(log_a)
        w = jnp.tanh(log_a) * (-1.0 - a * a)
        mult = jnp.where(w > 0.0, w * lax.rsqrt(w), 0.0)
        mult = jnp.where(start_mask, 1.0, mult)
        a_s[...] = a
        u_s[...] = xc * gate_x * mult

    xsf_s[halo:halo + rows, :] = xf_ref[...]
    yf = cb_ref[0] + cw_ref[0, 0] * xsf_s[halo:halo + rows, :]
    for k in range(1, RNN_CONV_W):
        yf = yf + cw_ref[0, k] * xsf_s[halo - k * nb:halo - k * nb + rows, :]
    xsf_s[0:halo, :] = xsf_s[rows:rows + halo, :]
    gates(0, yf, (row < nb) & (i == 0), af_s, uf_s)

    xsb_s[0:rows, :] = xb_ref[...]
    yb = cb_ref[1] + cw_ref[1, 0] * xsb_s[0:rows, :]
    for k in range(1, RNN_CONV_W):
        yb = yb + cw_ref[1, k] * xsb_s[k * nb:k * nb + rows, :]
    xsb_s[rows:rows + halo, :] = xsb_s[0:halo, :]
    gates(1, yb, (row >= rows - nb) & (i == 0), ab_s, ub_s)

    def step(tt, carry):
        cf, cb = carry
        rf = pl.ds(pl.multiple_of(tt * nb, nb), nb)
        rb = pl.ds(pl.multiple_of((ts - 1 - tt) * nb, nb), nb)
        cf = af_s[rf, :] * cf + uf_s[rf, :]
        cb = ab_s[rb, :] * cb + ub_s[rb, :]
        hf_ref[rf, :] = cf
        hb_ref[rb, :] = cb
        return cf, cb

    cf, cb = lax.fori_loop(0, ts, step, (carry_s[0], carry_s[1]), unroll=8)
    carry_s[0] = cf
    carry_s[1] = cb


def _rnn(xr, l, conv_w, conv_b, wg, b_a, b_x, lam, nb, ts):
    m, d = xr.shape
    bw = d // N_RNN_BLOCKS
    rows = ts * nb
    nt = m // rows
    halo = (RNN_CONV_W - 1) * nb
    xs = pltpu.VMEM((rows + halo, bw), F32)
    au = pltpu.VMEM((rows, bw), F32)
    vec = lambda: pl.BlockSpec((None, 2, 1, bw), lambda c, i: (l, 0, 0, c))
    return pl.pallas_call(
        functools.partial(_rnn_kernel, nb=nb),
        grid=(N_RNN_BLOCKS, nt),
        in_specs=[pl.BlockSpec((rows, bw), lambda c, i: (i, c)),
                  pl.BlockSpec((rows, bw), lambda c, i: (nt - 1 - i, c)),
                  pl.BlockSpec((None, 2, RNN_CONV_W, 1, bw), lambda c, i: (l, 0, 0, 0, c)),
                  vec(),
                  pl.BlockSpec((None, 2, 1, bw, 2 * bw), lambda c, i: (l, 0, c, 0, 0)),
                  vec(), vec(), vec()],
        out_specs=[pl.BlockSpec((rows, bw), lambda c, i: (i, c)),
                   pl.BlockSpec((rows, bw), lambda c, i: (nt - 1 - i, c))],
        out_shape=[jax.ShapeDtypeStruct((m, d), F32)] * 2,
        scratch_shapes=[xs, xs, au, au, au, au, pltpu.VMEM((2, nb, bw), F32)],
        compiler_params=_cparams("parallel", "arbitrary"),
        name="rg_lru",
    )(xr, xr, conv_w, conv_b, wg, b_a, b_x, lam)


def _attn_kernel(sink_ref, q_ref, k_ref, v_ref, o_ref, kp_s, vp_s, s_s, p_s, *, l):
    hk = pl.program_id(1)
    seq = q_ref.shape[1]
    blk = WINDOW
    nblk = seq // blk
    span = 3 * blk
    c1 = HEAD_DIM ** -0.5 * LOG2E

    kp_s[0:blk, :] = jnp.zeros((blk, HEAD_DIM), BF16)
    kp_s[blk:blk + seq, :] = k_ref[0]
    kp_s[blk + seq:2 * blk + seq, :] = jnp.zeros((blk, HEAD_DIM), BF16)
    vp_s[0:blk, 0:HEAD_DIM] = jnp.zeros((blk, HEAD_DIM), BF16)
    vp_s[blk:blk + seq, 0:HEAD_DIM] = v_ref[0]
    vp_s[blk + seq:2 * blk + seq, 0:HEAD_DIM] = jnp.zeros((blk, HEAD_DIM), BF16)
    vp_s[:, HEAD_DIM:] = jnp.ones((seq + 2 * blk, HEAD_DIM), BF16)

    qi = lax.broadcasted_iota(jnp.int32, (blk, blk), 0)
    kj = lax.broadcasted_iota(jnp.int32, (blk, blk), 1)
    bias_prev = jnp.where(kj >= qi, 0.0, NEG_INF)
    bias_next = jnp.where(kj <= qi, 0.0, NEG_INF)

    def logits(n, slot):
        q0 = pl.multiple_of(n * blk, blk)
        qs = jnp.concatenate(
            [q_ref[0, pl.ds(q0, blk), g * HEAD_DIM:(g + 1) * HEAD_DIM] for g in range(Q_GROUP)], axis=0)
        s_s[slot] = lax.dot_general(qs, kp_s[pl.ds(q0, span), :], (((1,), (1,)), ((), ())),
                                    preferred_element_type=F32)

    def softmax(n, slot):
        bp = jnp.where(n == 0, NEG_INF, bias_prev)
        bn = jnp.where(n == nblk - 1, NEG_INF, bias_next)
        m2s = []
        for g in range(Q_GROUP):
            r = slice(g * blk, (g + 1) * blk)
            s_prev = s_s[slot, r, 0:blk] + bp
            s_cur = s_s[slot, r, blk:2 * blk]
            s_next = s_s[slot, r, 2 * blk:span] + bn
            m_raw = jnp.max(jnp.maximum(jnp.maximum(s_prev, s_cur), s_next), axis=-1, keepdims=True)
            m2 = jnp.maximum(m_raw * c1, sink_ref[l, hk * Q_GROUP + g] * LOG2E)
            p_s[slot, r, 0:blk] = jnp.exp2(s_prev * c1 - m2).astype(BF16)
            p_s[slot, r, blk:2 * blk] = jnp.exp2(s_cur * c1 - m2).astype(BF16)
            p_s[slot, r, 2 * blk:span] = jnp.exp2(s_next * c1 - m2).astype(BF16)
            m2s.append(m2)
        return tuple(m2s)

    def values(n, slot, m2s):
        q0 = n * blk if isinstance(n, int) else pl.multiple_of(n * blk, blk)
        oe = jnp.dot(p_s[slot], vp_s[pl.ds(q0, span), :], preferred_element_type=F32)
        for g in range(Q_GROUP):
            r = slice(g * blk, (g + 1) * blk)
            den = oe[r, HEAD_DIM:] + jnp.exp2(sink_ref[l, hk * Q_GROUP + g] * LOG2E - m2s[g])
            o_ref[0, pl.ds(q0, blk), g * HEAD_DIM:(g + 1) * HEAD_DIM] = (oe[r, :HEAD_DIM] / den).astype(BF16)

    logits(0, 0)
    p_s[1] = jnp.zeros(p_s.shape[1:], BF16)

    def body(i, m_prev):
        n = 2 * i
        logits(n + 1, 1)
        m_even = softmax(n, 0)
        values(jnp.maximum(n - 1, 0), 1, m_prev)
        logits(jnp.minimum(n + 2, nblk - 1), 0)
        m_odd = softmax(n + 1, 1)
        values(n, 0, m_even)
        return m_odd

    assert nblk % 2 == 0
    m_init = tuple(jnp.zeros((blk, 1), F32) for _ in range(Q_GROUP))
    m_last = lax.fori_loop(0, nblk // 2, body, m_init)
    values(nblk - 1, 1, m_last)


def _attention(qkv3, l, sink):
    nb, seq, _ = qkv3.shape
    gw = Q_GROUP * HEAD_DIM
    k_blk0 = N_HEADS
    v_blk0 = N_HEADS + N_KV_HEADS
    rows = Q_GROUP * WINDOW
    return pl.pallas_call(
        functools.partial(_attn_kernel, l=l),
        grid=(nb, N_KV_HEADS),
        in_specs=[pl.BlockSpec(memory_space=pltpu.SMEM),
                  pl.BlockSpec((1, seq, gw), lambda b, h: (b, 0, h)),
                  pl.BlockSpec((1, seq, HEAD_DIM), lambda b, h: (b, 0, k_blk0 + h)),
                  pl.BlockSpec((1, seq, HEAD_DIM), lambda b, h: (b, 0, v_blk0 + h))],
        out_specs=pl.BlockSpec((1, seq, gw), lambda b, h: (b, 0, h)),
        out_shape=jax.ShapeDtypeStruct((nb, seq, N_HEADS * HEAD_DIM), BF16),
        scratch_shapes=[pltpu.VMEM((seq + 2 * WINDOW, HEAD_DIM), BF16),
                        pltpu.VMEM((seq + 2 * WINDOW, 2 * HEAD_DIM), BF16),
                        pltpu.VMEM((2, rows, 3 * WINDOW), F32),
                        pltpu.VMEM((2, rows, 3 * WINDOW), BF16)],
        compiler_params=_cparams("parallel", "parallel"),
        name="swa_attention",
    )(sink, qkv3, qkv3, qkv3)


def _merge_kernel(hb_ref, attn_ref, hf_ref, hbk_ref, h_ref,
                  wz_ref, wpa_ref, wpr_ref, wo_ref, g_ref, b_ref, o_ref, ob_ref, at_s, *, alpha):
    d = h_ref.shape[1]
    nb = attn_ref.shape[0]
    z = jnp.dot(hb_ref[...], wz_ref[...], preferred_element_type=F32)
    rnn = (hf_ref[...] + hbk_ref[...]) * _gelu(z[:, :d])
    for b in range(nb):
        _put_sequence(at_s, b, attn_ref[b].astype(F32))
    pa = jnp.dot(_slabs(at_s).astype(BF16), wpa_ref[...], preferred_element_type=F32)
    pr = jnp.dot(rnn.astype(BF16), wpr_ref[...], preferred_element_type=F32)
    merged = jax.nn.sigmoid(z[:, d:2 * d]) * pa + jax.nn.sigmoid(z[:, 2 * d:]) * pr
    o = jnp.dot(merged.astype(BF16), wo_ref[...], preferred_element_type=F32)
    y = _layer_norm(alpha * h_ref[...] + o, g_ref[...], b_ref[...])
    o_ref[...] = y
    ob_ref[...] = y.astype(BF16)


def _merge(hb, attn3, hf, hbk, h, l, wz, wpa, wpr, wo, g, b, alpha, tm):
    m, d = h.shape
    nb = attn3.shape[0]
    row = pl.BlockSpec((tm, d), lambda i: (i, 0))
    return pl.pallas_call(
        functools.partial(_merge_kernel, alpha=alpha),
        grid=(m // tm,),
        in_specs=[row, pl.BlockSpec((nb, tm // nb, d), lambda i: (0, i, 0)), row, row, row]
                 + [_layer_spec(a, l) for a in (wz, wpa, wpr, wo, g, b)],
        out_specs=[row, row],
        out_shape=[jax.ShapeDtypeStruct((m, d), F32), jax.ShapeDtypeStruct((m, d), BF16)],
        scratch_shapes=[_slab_scratch(tm, d)],
        compiler_params=_cparams("parallel"),
        name="merge_out_ln",
    )(hb, attn3, hf, hbk, h, wz, wpa, wpr, wo, g, b)


def _ffn_kernel(hb_ref, hprev_ref, hnext_ref, h_ref, p_ref, wu_ref, wd_ref, cw_ref, cb_ref,
                wpg_ref, wple_ref, g_ref, b_ref, *rest, alpha, tf, nb, batch_major_out):
    if batch_major_out:
        o_ref, p_s, y_s = rest
    else:
        o_ref, ob_ref, p_s = rest
    i = pl.program_id(0)
    tm = hb_ref.shape[0]
    dff = wd_ref.shape[0]
    halo = hprev_ref.shape[0]
    hb = hb_ref[...]
    hp = hprev_ref[...]
    hn = hnext_ref[...]
    hp = jnp.where(i == 0, jnp.zeros_like(hp), hp)
    hn = jnp.where(i == pl.num_programs(0) - 1, jnp.zeros_like(hn), hn)
    lhs = jnp.concatenate([hp, hb, hn], axis=0)
    acc = None
    for j in range(dff // tf):
        cs = slice(j * tf, (j + 1) * tf)
        gate = jnp.dot(lhs, wu_ref[:, cs], preferred_element_type=F32)
        val = jnp.dot(hb, wu_ref[:, dff + j * tf:dff + (j + 1) * tf], preferred_element_type=F32)
        conv = (cw_ref[0:1, cs] * gate[halo - nb:halo - nb + tm] + cw_ref[1:2, cs] * gate[halo:halo + tm]
                + cw_ref[2:3, cs] * gate[halo + nb:halo + nb + tm] + cb_ref[:, cs])
        act = (_gelu(conv) * val).astype(BF16)
        pc = jnp.dot(act, wd_ref[cs, :], preferred_element_type=F32)
        acc = pc if acc is None else acc + pc

    for b in range(nb):
        _put_sequence(p_s, b, p_ref[b])
    ple = (jax.nn.sigmoid(jnp.dot(hb, wpg_ref[...], preferred_element_type=F32))
           * jnp.dot(_slabs(p_s).astype(BF16), wple_ref[...], preferred_element_type=F32))
    y = _layer_norm(alpha * h_ref[...] + acc + ple, g_ref[...], b_ref[...])
    if batch_major_out:
        for c in range(y_s.shape[0]):
            y_s[c] = y[:, c * LANES:(c + 1) * LANES]
        for b in range(nb):
            for c in range(y_s.shape[0]):
                o_ref[b, :, c * LANES:(c + 1) * LANES] = _get_sequence(y_s, b, tm // nb, c)
    else:
        o_ref[...] = y
        ob_ref[...] = y.astype(BF16)


def _ffn(h, hb, l, p4, w_up, w_down, cw, cb, wpg, wple, g, b, alpha, tm, tf, batch_major_out):
    m, d = h.shape
    _, nb, seq, dple = p4.shape
    tt = tm // nb
    halo = BF16_ROWS
    per = tm // halo
    last = m // halo - 1
    row = pl.BlockSpec((tm, d), lambda i: (i, 0))
    if batch_major_out:
        out_specs = pl.BlockSpec((nb, tt, d), lambda i: (0, i, 0))
        out_shape = jax.ShapeDtypeStruct((nb, seq, d), F32)
        scratch = [_slab_scratch(tm, dple), _slab_scratch(tm, d)]
    else:
        out_specs = [row, row]
        out_shape = [jax.ShapeDtypeStruct((m, d), F32), jax.ShapeDtypeStruct((m, d), BF16)]
        scratch = [_slab_scratch(tm, dple)]
    return pl.pallas_call(
        functools.partial(_ffn_kernel, alpha=alpha, tf=tf, nb=nb, batch_major_out=batch_major_out),
        grid=(m // tm,),
        in_specs=[row,
                  pl.BlockSpec((halo, d), lambda i: (jnp.maximum(i * per - 1, 0), 0)),
                  pl.BlockSpec((halo, d), lambda i: (jnp.minimum((i + 1) * per, last), 0)),
                  row,
                  pl.BlockSpec((None, nb, tt, dple), lambda i: (l, 0, i, 0))]
                 + [_layer_spec(a, l) for a in (w_up, w_down, cw, cb, wpg, wple, g, b)],
        out_specs=out_specs,
        out_shape=out_shape,
        scratch_shapes=scratch,
        compiler_params=_cparams("parallel"),
        name="ffn_ple_ln",
    )(hb, hb, hb, h, p4, w_up, w_down, cw, cb, wpg, wple, g, b)


def _rope_head_perm():
    half = ROPE_DIM // 2
    mid = HEAD_DIM // 2
    rest = list(range(ROPE_DIM, HEAD_DIM))
    return (list(range(half)) + rest[:mid - half] + list(range(half, ROPE_DIM)) + rest[mid - half:])


def _rope_tables(seq):
    half = ROPE_DIM // 2
    mid = HEAD_DIM // 2
    pos = jnp.arange(seq, dtype=F32)
    inv = ROPE_THETA ** (-jnp.arange(0, ROPE_DIM, 2, dtype=F32) / ROPE_DIM)
    ang = pos[:, None] * inv[None, :]
    cos, sin = jnp.cos(ang), jnp.sin(ang)
    ones = jnp.ones((seq, mid - half), F32)
    zeros = jnp.zeros((seq, mid - half), F32)
    c = jnp.concatenate([cos, ones, cos, ones], axis=1)
    s = jnp.concatenate([-sin, zeros, sin, zeros], axis=1)
    return c, s


def kernel(x, p, ln_in_g, ln_in_b, w_in, attn_sink, rnn_conv_w, rnn_conv_b, rg_w_a, rg_b_a, rg_w_x, rg_b_x, rg_lambda, w_proj_attn, w_proj_rnn, w_out, ln1_g, ln1_b, w_up, ffn_conv_w, ffn_conv_b, w_down, w_ple, w_ple_gate, ln2_g, ln2_b):
    nb, seq, d = x.shape
    depth = w_in.shape[0]
    m = nb * seq
    alpha = float((2 * depth) ** 0.25)
    n_qkv = (N_HEADS + 2 * N_KV_HEADS) * HEAD_DIM
    assert nb == SUBLANES and d == N_HEADS * HEAD_DIM and seq % WINDOW == 0 and seq >= 3 * WINDOW

    tm_ln = min(TM_LN, m)
    tm_proj = min(TM_PROJ, m)
    tm_merge = min(TM_MERGE, m)
    tm_ffn = min(TM_FFN, m)
    ts_rnn = min(TS_RNN, seq)

    head_perm = _rope_head_perm()
    qkv_cols = [hd * HEAD_DIM + c for hd in range(N_HEADS + N_KV_HEADS) for c in head_perm]
    qkv_cols += list(range(len(qkv_cols), n_qkv))
    w_qkv = jnp.take(w_in, jnp.asarray(qkv_cols, jnp.int32), axis=2).astype(BF16)
    w_xr = w_in[:, :, n_qkv:n_qkv + d].astype(BF16)
    w_z = w_in[:, :, n_qkv + d:].astype(BF16)
    w_g = jnp.concatenate([rg_w_a, rg_w_x], axis=-1).astype(BF16)
    w_pa, w_pr, w_o = w_proj_attn.astype(BF16), w_proj_rnn.astype(BF16), w_out.astype(BF16)
    w_u, w_d = w_up.astype(BF16), w_down.astype(BF16)
    w_pg, w_pl = w_ple_gate.astype(BF16), w_ple.astype(BF16)

    row = lambda a: a.reshape(a.shape[:-1] + (1, a.shape[-1]))
    conv_w, conv_b = row(rnn_conv_w), row(rnn_conv_b)
    b_a, b_x, lam = row(rg_b_a), row(rg_b_x), row(rg_lambda)
    g1, b1, g2, b2, ffn_cb = row(ln1_g), row(ln1_b), row(ln2_g), row(ln2_b), row(ffn_conv_b)

    rope_c, rope_s = _rope_tables(seq)
    h, hb = _ln_in(x, ln_in_g, ln_in_b, tm_ln)

    for l in range(depth):
        qkv = _qkv_proj(hb, l, w_qkv, rope_c, rope_s, nb, tm_proj, TN)
        xr = _xr_proj(hb, l, w_xr, tm_proj, TN)
        hf, hbk = _rnn(xr, l, conv_w, conv_b, w_g, b_a, b_x, lam, nb, ts_rnn)
        attn = _attention(qkv, l, attn_sink)
        h, hb = _merge(hb, attn, hf, hbk, h, l, w_z, w_pa, w_pr, w_o, g1, b1, alpha, tm_merge)
        out = _ffn(h, hb, l, p, w_u, w_d, ffn_conv_w, ffn_cb, w_pg, w_pl, g2, b2,
                   alpha, tm_ffn, TN, batch_major_out=(l == depth - 1))
        if l < depth - 1:
            h, hb = out
    return out
```

```python
import functools
import math

import jax
import jax.numpy as jnp
from jax import lax
from jax.experimental import pallas as pl
from jax.experimental.pallas import tpu as pltpu

F32 = jnp.float32
BF16 = jnp.bfloat16

N_HEADS = 8
N_KV_HEADS = 2
HEAD_DIM = 128
Q_GROUP = N_HEADS // N_KV_HEADS
WINDOW = 128
ROPE_THETA = 500000.0
ROPE_DIM = HEAD_DIM // 4
N_RNN_BLOCKS = 4
RNN_CONV_W = 4
FFN_CONV_W = 3
RG_C = 8.0
LN_EPS = 1e-5
NEG_INF = -1e30
LOG2E = math.log2(math.e)
GELU_C = math.sqrt(2.0 / math.pi)
GELU_A = 0.044715

LANES = 128
SUBLANES = 8
BF16_ROWS = 16
MXU_COLS = 256
VMEM_LIMIT = 56 * 1024 * 1024

TM_LN = 1024
TM_PROJ = 2048
TM_MERGE = 512
TM_FFN = 512
TS_RNN = 256
TN = 512
TR_CAST = 256


def _cparams(*sem):
    return pltpu.CompilerParams(dimension_semantics=sem, vmem_limit_bytes=VMEM_LIMIT)


def _layer_norm(x, g, b):
    mu = jnp.mean(x, axis=-1, keepdims=True)
    xc = x - mu
    var = jnp.mean(xc * xc, axis=-1, keepdims=True)
    return xc * lax.rsqrt(var + LN_EPS) * g + b


def _gelu(x):
    k1 = -2.0 * GELU_C * LOG2E
    return x / (1.0 + jnp.exp2(x * (k1 + (k1 * GELU_A) * (x * x))))


def _softplus(x):
    return jnp.maximum(x, 0.0) + jnp.log1p(jnp.exp(-jnp.abs(x)))


def _const_spec(shape):
    nd = len(shape)
    return pl.BlockSpec(shape, lambda *_: (0,) * nd, pipeline_mode=pl.Buffered(1))


def _layer_spec(arr, l):
    tail = (0,) * (arr.ndim - 1)
    return pl.BlockSpec((None,) + arr.shape[1:], lambda *_: (l,) + tail, pipeline_mode=pl.Buffered(1))


def _slab_scratch(rows, cols):
    return pltpu.VMEM((cols // LANES, rows, LANES), F32)


def _put_sequence(slab_s, b, val):
    tt, n = val.shape
    for c in range(n // LANES):
        slab_s[c, pl.ds(b, tt, stride=SUBLANES), :] = val[:, c * LANES:(c + 1) * LANES]


def _get_sequence(slab_s, b, tt, c):
    return slab_s[c, pl.ds(b, tt, stride=SUBLANES), :]


def _slabs(slab_s):
    return jnp.concatenate([slab_s[c] for c in range(slab_s.shape[0])], axis=1)


def _ln_in_kernel(x_ref, g_ref, b_ref, h_ref, hb_ref, t_s):
    nb = x_ref.shape[0]
    for b in range(nb):
        _put_sequence(t_s, b, _layer_norm(x_ref[b], g_ref[...], b_ref[...]))
    y = _slabs(t_s)
    h_ref[...] = y
    hb_ref[...] = y.astype(BF16)


def _ln_in(x, g, b, tm):
    nb, seq, d = x.shape
    m = nb * seq
    tt = tm // nb
    return pl.pallas_call(
        _ln_in_kernel,
        grid=(m // tm,),
        in_specs=[pl.BlockSpec((nb, tt, d), lambda i: (0, i, 0)),
                  _const_spec((1, d)), _const_spec((1, d))],
        out_specs=[pl.BlockSpec((tm, d), lambda i: (i, 0)),
                   pl.BlockSpec((tm, d), lambda i: (i, 0))],
        out_shape=[jax.ShapeDtypeStruct((m, d), F32), jax.ShapeDtypeStruct((m, d), BF16)],
        scratch_shapes=[_slab_scratch(tm, d)],
        compiler_params=_cparams("parallel"),
        name="ln_in",
    )(x, g.reshape(1, d), b.reshape(1, d))


def _w_in_cast_kernel(w_ref, wa_ref, wz_ref):
    half = ROPE_DIM // 2
    mid = HEAD_DIM // 2
    pieces = ((0, half), (ROPE_DIM, mid + half), (half, ROPE_DIM), (mid + half, HEAD_DIM))
    na = wa_ref.shape[1]
    for hd in range(N_HEADS + N_KV_HEADS):
        x = w_ref[:, hd * HEAD_DIM:(hd + 1) * HEAD_DIM]
        y = jnp.concatenate([x[:, a:b] for a, b in pieces], axis=1)
        wa_ref[:, hd * HEAD_DIM:(hd + 1) * HEAD_DIM] = y.astype(BF16)
    rest = (N_HEADS + N_KV_HEADS) * HEAD_DIM
    wa_ref[:, rest:] = w_ref[:, rest:na].astype(BF16)
    wz_ref[...] = w_ref[:, na:].astype(BF16)


def _w_in_cast(w_in, na, tr):
    depth, d, n = w_in.shape
    return pl.pallas_call(
        _w_in_cast_kernel,
        grid=(depth, d // tr),
        in_specs=[pl.BlockSpec((None, tr, n), lambda l, i: (l, i, 0))],
        out_specs=[pl.BlockSpec((None, tr, na), lambda l, i: (l, i, 0)),
                   pl.BlockSpec((None, tr, n - na), lambda l, i: (l, i, 0))],
        out_shape=[jax.ShapeDtypeStruct((depth, d, na), BF16),
                   jax.ShapeDtypeStruct((depth, d, n - na), BF16)],
        compiler_params=_cparams("parallel", "parallel"),
        name="w_in_cast",
    )(w_in)


def _in_proj_kernel(h_ref, w_ref, c_ref, s_ref, o_ref, xr_ref, z_s, *, n_qkv_tiles, n_rope_last):
    j = pl.program_id(1)
    last = n_qkv_tiles - 1
    nb, tt, _ = o_ref.shape
    h = h_ref[...]

    @pl.when(j >= n_qkv_tiles)
    def _():
        xr_ref[...] = jnp.dot(h, w_ref[...], preferred_element_type=F32)

    @pl.when(j < n_qkv_tiles)
    def _():
        c, s = c_ref[...], s_ref[...]

        def rope(zc):
            return zc * c + pltpu.roll(zc, LANES // 2, 1) * s

        heads_per_dot = MXU_COLS // LANES
        for cd in range(w_ref.shape[1] // MXU_COLS):
            z = jnp.dot(h, w_ref[:, cd * MXU_COLS:(cd + 1) * MXU_COLS], preferred_element_type=F32)
            for ch in range(heads_per_dot):
                z_s[cd * heads_per_dot + ch] = z[:, ch * LANES:(ch + 1) * LANES]

        def emit(cc, rotary):
            for b in range(nb):
                zc = _get_sequence(z_s, b, tt, cc)
                o_ref[b, :, cc * LANES:(cc + 1) * LANES] = (rope(zc) if rotary else zc).astype(BF16)

        for cc in range(z_s.shape[0]):
            if cc < n_rope_last:
                emit(cc, True)
            else:
                pl.when(j < last)(functools.partial(emit, cc, True))
                pl.when(j == last)(functools.partial(emit, cc, False))


def _in_proj(hb, l, w, rope_c, rope_s, nb, n_qkv, tm, tn):
    m, d = hb.shape
    n = w.shape[2]
    seq = m // nb
    tt = tm // nb
    n_rope_cols = (N_HEADS + N_KV_HEADS) * HEAD_DIM
    assert n % tn == 0 and n_qkv % tn == 0 and tn % MXU_COLS == 0 and (n_qkv - tn) <= n_rope_cols <= n_qkv
    nq = n_qkv // tn
    n_rope_last = (n_rope_cols - (n_qkv - tn)) // LANES
    tab = pl.BlockSpec((tt, LANES), lambda i, j: (i, 0))
    return pl.pallas_call(
        functools.partial(_in_proj_kernel, n_qkv_tiles=nq, n_rope_last=n_rope_last),
        grid=(m // tm, n // tn),
        in_specs=[pl.BlockSpec((tm, d), lambda i, j: (i, 0)),
                  pl.BlockSpec((None, d, tn), lambda i, j: (l, 0, j)),
                  tab, tab],
        out_specs=[pl.BlockSpec((nb, tt, tn), lambda i, j: (0, i, jnp.minimum(j, nq - 1))),
                   pl.BlockSpec((tm, tn), lambda i, j: (i, jnp.maximum(j - nq, 0)))],
        out_shape=[jax.ShapeDtypeStruct((nb, seq, n_qkv), BF16),
                   jax.ShapeDtypeStruct((m, n - n_qkv), F32)],
        scratch_shapes=[_slab_scratch(tm, tn)],
        compiler_params=_cparams("parallel", "arbitrary"),
        name="in_proj",
    )(hb, w, rope_c, rope_s)


def _rnn_kernel(xf_ref, xb_ref, cw_ref, cb_ref, wg_ref, ba_ref, bx_ref, lam_ref,
                hf_ref, hb_ref, xsf_s, xsb_s, af_s, uf_s, ab_s, ub_s, carry_s, *, nb):
    i = pl.program_id(1)
    rows, bw = hf_ref.shape
    ts = rows // nb
    halo = (RNN_CONV_W - 1) * nb

    @pl.when(i == 0)
    def _():
        xsf_s[0:halo, :] = jnp.zeros((halo, bw), F32)
        xsb_s[rows:rows + halo, :] = jnp.zeros((halo, bw), F32)
        carry_s[...] = jnp.zeros_like(carry_s)

    row = lax.broadcasted_iota(jnp.int32, (rows, bw), 0)

    def gates(d, xc, start_mask, a_s, u_s):
        g = jnp.dot(xc.astype(BF16), wg_ref[d, 0], preferred_element_type=F32)
        gate_a = 1.0 / (1.0 + jnp.exp2(g[:, :bw] * (-LOG2E) + ba_ref[d] * (-LOG2E)))
        gate_x = 1.0 / (1.0 + jnp.exp2(g[:, bw:] * (-LOG2E) + bx_ref[d] * (-LOG2E)))
        log_a = gate_a * ((-RG_C) * _softplus(-lam_ref[d]))
        a = jnp.exp(log_a)
        w = jnp.tanh(log_a) * (-1.0 - a * a)
        mult = jnp.where(w > 0.0, w * lax.rsqrt(w), 0.0)
        mult = jnp.where(start_mask, 1.0, mult)
        a_s[...] = a
        u_s[...] = xc * gate_x * mult

    xsf_s[halo:halo + rows, :] = xf_ref[...]
    yf = cb_ref[0] + cw_ref[0, 0] * xsf_s[halo:halo + rows, :]
    for k in range(1, RNN_CONV_W):
        yf = yf + cw_ref[0, k] * xsf_s[halo - k * nb:halo - k * nb + rows, :]
    xsf_s[0:halo, :] = xsf_s[rows:rows + halo, :]
    gates(0, yf, (row < nb) & (i == 0), af_s, uf_s)

    xsb_s[0:rows, :] = xb_ref[...]
    yb = cb_ref[1] + cw_ref[1, 0] * xsb_s[0:rows, :]
    for k in range(1, RNN_CONV_W):
        yb = yb + cw_ref[1, k] * xsb_s[k * nb:k * nb + rows, :]
    xsb_s[rows:rows + halo, :] = xsb_s[0:halo, :]
    gates(1, yb, (row >= rows - nb) & (i == 0), ab_s, ub_s)

    def step(tt, carry):
        cf, cb = carry
        rf = pl.ds(pl.multiple_of(tt * nb, nb), nb)
        rb = pl.ds(pl.multiple_of((ts - 1 - tt) * nb, nb), nb)
        cf = af_s[rf, :] * cf + uf_s[rf, :]
        cb = ab_s[rb, :] * cb + ub_s[rb, :]
        hf_ref[rf, :] = cf
        hb_ref[rb, :] = cb
        return cf, cb

    cf, cb = lax.fori_loop(0, ts, step, (carry_s[0], carry_s[1]), unroll=8)
    carry_s[0] = cf
    carry_s[1] = cb


def _rnn(xr, l, conv_w, conv_b, wg, b_a, b_x, lam, nb, ts):
    m, d = xr.shape
    bw = d // N_RNN_BLOCKS
    rows = ts * nb
    nt = m // rows
    halo = (RNN_CONV_W - 1) * nb
    xs = pltpu.VMEM((rows + halo, bw), F32)
    au = pltpu.VMEM((rows, bw), F32)
    vec = lambda: pl.BlockSpec((None, 2, 1, bw), lambda c, i: (l, 0, 0, c))
    return pl.pallas_call(
        functools.partial(_rnn_kernel, nb=nb),
        grid=(N_RNN_BLOCKS, nt),
        in_specs=[pl.BlockSpec((rows, bw), lambda c, i: (i, c)),
                  pl.BlockSpec((rows, bw), lambda c, i: (nt - 1 - i, c)),
                  pl.BlockSpec((None, 2, RNN_CONV_W, 1, bw), lambda c, i: (l, 0, 0, 0, c)),
                  vec(),
                  pl.BlockSpec((None, 2, 1, bw, 2 * bw), lambda c, i: (l, 0, c, 0, 0)),
                  vec(), vec(), vec()],
        out_specs=[pl.BlockSpec((rows, bw), lambda c, i: (i, c)),
                   pl.BlockSpec((rows, bw), lambda c, i: (nt - 1 - i, c))],
        out_shape=[jax.ShapeDtypeStruct((m, d), F32)] * 2,
        scratch_shapes=[xs, xs, au, au, au, au, pltpu.VMEM((2, nb, bw), F32)],
        compiler_params=_cparams("parallel", "arbitrary"),
        name="rg_lru",
    )(xr, xr, conv_w, conv_b, wg, b_a, b_x, lam)


def _attn_kernel(sink_ref, q_ref, k_ref, v_ref, o_ref, kp_s, vp_s, s_s, p_s, *, l):
    hk = pl.program_id(1)
    seq = q_ref.shape[1]
    blk = WINDOW
    nblk = seq // blk
    span = 3 * blk
    c1 = HEAD_DIM ** -0.5 * LOG2E

    kp_s[0:blk, :] = jnp.zeros((blk, HEAD_DIM), BF16)
    kp_s[blk:blk + seq, :] = k_ref[0]
    kp_s[blk + seq:2 * blk + seq, :] = jnp.zeros((blk, HEAD_DIM), BF16)
    vp_s[0:blk, 0:HEAD_DIM] = jnp.zeros((blk, HEAD_DIM), BF16)
    vp_s[blk:blk + seq, 0:HEAD_DIM] = v_ref[0]
    vp_s[blk + seq:2 * blk + seq, 0:HEAD_DIM] = jnp.zeros((blk, HEAD_DIM), BF16)
    vp_s[:, HEAD_DIM:] = jnp.ones((seq + 2 * blk, HEAD_DIM), BF16)

    qi = lax.broadcasted_iota(jnp.int32, (blk, blk), 0)
    kj = lax.broadcasted_iota(jnp.int32, (blk, blk), 1)
    bias_prev = jnp.where(kj >= qi, 0.0, NEG_INF)
    bias_next = jnp.where(kj <= qi, 0.0, NEG_INF)

    def logits(n, slot):
        q0 = pl.multiple_of(n * blk, blk)
        qs = jnp.concatenate(
            [q_ref[0, pl.ds(q0, blk), g * HEAD_DIM:(g + 1) * HEAD_DIM] for g in range(Q_GROUP)], axis=0)
        s_s[slot] = lax.dot_general(qs, kp_s[pl.ds(q0, span), :], (((1,), (1,)), ((), ())),
                                    preferred_element_type=F32)

    def softmax(n, slot):
        bp = jnp.where(n == 0, NEG_INF, bias_prev)
        bn = jnp.where(n == nblk - 1, NEG_INF, bias_next)
        m2s = []
        for g in range(Q_GROUP):
            r = slice(g * blk, (g + 1) * blk)
            s_prev = s_s[slot, r, 0:blk] + bp
            s_cur = s_s[slot, r, blk:2 * blk]
            s_next = s_s[slot, r, 2 * blk:span] + bn
            m_raw = jnp.max(jnp.maximum(jnp.maximum(s_prev, s_cur), s_next), axis=-1, keepdims=True)
            m2 = jnp.maximum(m_raw * c1, sink_ref[l, hk * Q_GROUP + g] * LOG2E)
            p_s[slot, r, 0:blk] = jnp.exp2(s_prev * c1 - m2).astype(BF16)
            p_s[slot, r, blk:2 * blk] = jnp.exp2(s_cur * c1 - m2).astype(BF16)
            p_s[slot, r, 2 * blk:span] = jnp.exp2(s_next * c1 - m2).astype(BF16)
            m2s.append(m2)
        return tuple(m2s)

    def values(n, slot, m2s):
        q0 = n * blk if isinstance(n, int) else pl.multiple_of(n * blk, blk)
        oe = jnp.dot(p_s[slot], vp_s[pl.ds(q0, span), :], preferred_element_type=F32)
        for g in range(Q_GROUP):
            r = slice(g * blk, (g + 1) * blk)
            den = oe[r, HEAD_DIM:] + jnp.exp2(sink_ref[l, hk * Q_GROUP + g] * LOG2E - m2s[g])
            o_ref[0, pl.ds(q0, blk), g * HEAD_DIM:(g + 1) * HEAD_DIM] = (oe[r, :HEAD_DIM] / den).astype(BF16)

    logits(0, 0)
    p_s[1] = jnp.zeros(p_s.shape[1:], BF16)

    def body(i, m_prev):
        n = 2 * i
        logits(n + 1, 1)
        m_even = softmax(n, 0)
        values(jnp.maximum(n - 1, 0), 1, m_prev)
        logits(jnp.minimum(n + 2, nblk - 1), 0)
        m_odd = softmax(n + 1, 1)
        values(n, 0, m_even)
        return m_odd

    assert nblk % 2 == 0
    m_init = tuple(jnp.zeros((blk, 1), F32) for _ in range(Q_GROUP))
    m_last = lax.fori_loop(0, nblk // 2, body, m_init)
    values(nblk - 1, 1, m_last)


def _attention(qkv3, l, sink):
    nb, seq, _ = qkv3.shape
    gw = Q_GROUP * HEAD_DIM
    k_blk0 = N_HEADS
    v_blk0 = N_HEADS + N_KV_HEADS
    rows = Q_GROUP * WINDOW
    return pl.pallas_call(
        functools.partial(_attn_kernel, l=l),
        grid=(nb, N_KV_HEADS),
        in_specs=[pl.BlockSpec(memory_space=pltpu.SMEM),
                  pl.BlockSpec((1, seq, gw), lambda b, h: (b, 0, h)),
                  pl.BlockSpec((1, seq, HEAD_DIM), lambda b, h: (b, 0, k_blk0 + h)),
                  pl.BlockSpec((1, seq, HEAD_DIM), lambda b, h: (b, 0, v_blk0 + h))],
        out_specs=pl.BlockSpec((1, seq, gw), lambda b, h: (b, 0, h)),
        out_shape=jax.ShapeDtypeStruct((nb, seq, N_HEADS * HEAD_DIM), BF16),
        scratch_shapes=[pltpu.VMEM((seq + 2 * WINDOW, HEAD_DIM), BF16),
                        pltpu.VMEM((seq + 2 * WINDOW, 2 * HEAD_DIM), BF16),
                        pltpu.VMEM((2, rows, 3 * WINDOW), F32),
                        pltpu.VMEM((2, rows, 3 * WINDOW), BF16)],
        compiler_params=_cparams("parallel", "parallel"),
        name="swa_attention",
    )(sink, qkv3, qkv3, qkv3)


def _merge_kernel(hb_ref, attn_ref, hf_ref, hbk_ref, h_ref,
                  wz_ref, wpa_ref, wpr_ref, wo_ref, g_ref, b_ref, o_ref, ob_ref, at_s, *, alpha):
    d = h_ref.shape[1]
    nb = attn_ref.shape[0]
    z = jnp.dot(hb_ref[...], wz_ref[...], preferred_element_type=F32)
    rnn = (hf_ref[...] + hbk_ref[...]) * _gelu(z[:, :d])
    for b in range(nb):
        _put_sequence(at_s, b, attn_ref[b].astype(F32))
    pa = jnp.dot(_slabs(at_s).astype(BF16), wpa_ref[...], preferred_element_type=F32)
    pr = jnp.dot(rnn.astype(BF16), wpr_ref[...], preferred_element_type=F32)
    merged = jax.nn.sigmoid(z[:, d:2 * d]) * pa + jax.nn.sigmoid(z[:, 2 * d:]) * pr
    o = jnp.dot(merged.astype(BF16), wo_ref[...], preferred_element_type=F32)
    y = _layer_norm(alpha * h_ref[...] + o, g_ref[...], b_ref[...])
    o_ref[...] = y
    ob_ref[...] = y.astype(BF16)


def _merge(hb, attn3, hf, hbk, h, l, wz, wpa, wpr, wo, g, b, alpha, tm):
    m, d = h.shape
    nb = attn3.shape[0]
    row = pl.BlockSpec((tm, d), lambda i: (i, 0))
    return pl.pallas_call(
        functools.partial(_merge_kernel, alpha=alpha),
        grid=(m // tm,),
        in_specs=[row, pl.BlockSpec((nb, tm // nb, d), lambda i: (0, i, 0)), row, row, row]
                 + [_layer_spec(a, l) for a in (wz, wpa, wpr, wo, g, b)],
        out_specs=[row, row],
        out_shape=[jax.ShapeDtypeStruct((m, d), F32), jax.ShapeDtypeStruct((m, d), BF16)],
        scratch_shapes=[_slab_scratch(tm, d)],
        compiler_params=_cparams("parallel"),
        name="merge_out_ln",
    )(hb, attn3, hf, hbk, h, wz, wpa, wpr, wo, g, b)


def _ffn_kernel(hb_ref, hprev_ref, hnext_ref, h_ref, p_ref, wu_ref, wd_ref, cw_ref, cb_ref,
                wpg_ref, wple_ref, g_ref, b_ref, *rest, alpha, tf, nb, batch_major_out):
    if batch_major_out:
        o_ref, p_s, y_s = rest
    else:
        o_ref, ob_ref, p_s = rest
    i = pl.program_id(0)
    tm = hb_ref.shape[0]
    dff = wd_ref.shape[0]
    halo = hprev_ref.shape[0]
    hb = hb_ref[...]
    hp = hprev_ref[...]
    hn = hnext_ref[...]
    hp = jnp.where(i == 0, jnp.zeros_like(hp), hp)
    hn = jnp.where(i == pl.num_programs(0) - 1, jnp.zeros_like(hn), hn)
    lhs = jnp.concatenate([hp, hb, hn], axis=0)
    acc = None
    for j in range(dff // tf):
        cs = slice(j * tf, (j + 1) * tf)
        gate = jnp.dot(lhs, wu_ref[:, cs], preferred_element_type=F32)
        val = jnp.dot(hb, wu_ref[:, dff + j * tf:dff + (j + 1) * tf], preferred_element_type=F32)
        conv = (cw_ref[0:1, cs] * gate[halo - nb:halo - nb + tm] + cw_ref[1:2, cs] * gate[halo:halo + tm]
                + cw_ref[2:3, cs] * gate[halo + nb:halo + nb + tm] + cb_ref[:, cs])
        act = (_gelu(conv) * val).astype(BF16)
        pc = jnp.dot(act, wd_ref[cs, :], preferred_element_type=F32)
        acc = pc if acc is None else acc + pc

    for b in range(nb):
        _put_sequence(p_s, b, p_ref[b])
    ple = (jax.nn.sigmoid(jnp.dot(hb, wpg_ref[...], preferred_element_type=F32))
           * jnp.dot(_slabs(p_s).astype(BF16), wple_ref[...], preferred_element_type=F32))
    y = _layer_norm(alpha * h_ref[...] + acc + ple, g_ref[...], b_ref[...])
    if batch_major_out:
        for c in range(y_s.shape[0]):
            y_s[c] = y[:, c * LANES:(c + 1) * LANES]
        for b in range(nb):
            for c in range(y_s.shape[0]):
                o_ref[b, :, c * LANES:(c + 1) * LANES] = _get_sequence(y_s, b, tm // nb, c)
    else:
        o_ref[...] = y
        ob_ref[...] = y.astype(BF16)


def _ffn(h, hb, l, p4, w_up, w_down, cw, cb, wpg, wple, g, b, alpha, tm, tf, batch_major_out):
    m, d = h.shape
    _, nb, seq, dple = p4.shape
    tt = tm // nb
    halo = BF16_ROWS
    per = tm // halo
    last = m // halo - 1
    row = pl.BlockSpec((tm, d), lambda i: (i, 0))
    if batch_major_out:
        out_specs = pl.BlockSpec((nb, tt, d), lambda i: (0, i, 0))
        out_shape = jax.ShapeDtypeStruct((nb, seq, d), F32)
        scratch = [_slab_scratch(tm, dple), _slab_scratch(tm, d)]
    else:
        out_specs = [row, row]
        out_shape = [jax.ShapeDtypeStruct((m, d), F32), jax.ShapeDtypeStruct((m, d), BF16)]
        scratch = [_slab_scratch(tm, dple)]
    return pl.pallas_call(
        functools.partial(_ffn_kernel, alpha=alpha, tf=tf, nb=nb, batch_major_out=batch_major_out),
        grid=(m // tm,),
        in_specs=[row,
                  pl.BlockSpec((halo, d), lambda i: (jnp.maximum(i * per - 1, 0), 0)),
                  pl.BlockSpec((halo, d), lambda i: (jnp.minimum((i + 1) * per, last), 0)),
                  row,
                  pl.BlockSpec((None, nb, tt, dple), lambda i: (l, 0, i, 0))]
                 + [_layer_spec(a, l) for a in (w_up, w_down, cw, cb, wpg, wple, g, b)],
        out_specs=out_specs,
        out_shape=out_shape,
        scratch_shapes=scratch,
        compiler_params=_cparams("parallel"),
        name="ffn_ple_ln",
    )(hb, hb, hb, h, p4, w_up, w_down, cw, cb, wpg, wple, g, b)


def _rope_tables(seq):
    half = ROPE_DIM // 2
    mid = HEAD_DIM // 2
    pos = jnp.arange(seq, dtype=F32)
    inv = ROPE_THETA ** (-jnp.arange(0, ROPE_DIM, 2, dtype=F32) / ROPE_DIM)
    ang = pos[:, None] * inv[None, :]
    cos, sin = jnp.cos(ang), jnp.sin(ang)
    ones = jnp.ones((seq, mid - half), F32)
    zeros = jnp.zeros((seq, mid - half), F32)
    c = jnp.concatenate([cos, ones, cos, ones], axis=1)
    s = jnp.concatenate([-sin, zeros, sin, zeros], axis=1)
    return c, s


def kernel(x, p, ln_in_g, ln_in_b, w_in, attn_sink, rnn_conv_w, rnn_conv_b, rg_w_a, rg_b_a, rg_w_x, rg_b_x, rg_lambda, w_proj_attn, w_proj_rnn, w_out, ln1_g, ln1_b, w_up, ffn_conv_w, ffn_conv_b, w_down, w_ple, w_ple_gate, ln2_g, ln2_b):
    nb, seq, d = x.shape
    depth = w_in.shape[0]
    m = nb * seq
    alpha = float((2 * depth) ** 0.25)
    n_qkv = (N_HEADS + 2 * N_KV_HEADS) * HEAD_DIM
    assert nb == SUBLANES and d == N_HEADS * HEAD_DIM and seq % WINDOW == 0 and seq >= 3 * WINDOW

    tm_ln = min(TM_LN, m)
    tm_proj = min(TM_PROJ, m)
    tm_merge = min(TM_MERGE, m)
    tm_ffn = min(TM_FFN, m)
    ts_rnn = min(TS_RNN, seq)

    w_a, w_z = _w_in_cast(w_in, n_qkv + d, min(TR_CAST, d))
    w_g = jnp.concatenate([rg_w_a, rg_w_x], axis=-1).astype(BF16)
    w_pa, w_pr, w_o = w_proj_attn.astype(BF16), w_proj_rnn.astype(BF16), w_out.astype(BF16)
    w_u, w_d = w_up.astype(BF16), w_down.astype(BF16)
    w_pg, w_pl = w_ple_gate.astype(BF16), w_ple.astype(BF16)

    row = lambda a: a.reshape(a.shape[:-1] + (1, a.shape[-1]))
    conv_w, conv_b = row(rnn_conv_w), row(rnn_conv_b)
    b_a, b_x, lam = row(rg_b_a), row(rg_b_x), row(rg_lambda)
    g1, b1, g2, b2, ffn_cb = row(ln1_g), row(ln1_b), row(ln2_g), row(ln2_b), row(ffn_conv_b)

    rope_c, rope_s = _rope_tables(seq)
    h, hb = _ln_in(x, ln_in_g, ln_in_b, tm_ln)

    for l in range(depth):
        qkv, xr = _in_proj(hb, l, w_a, rope_c, rope_s, nb, n_qkv, tm_proj, TN)
        hf, hbk = _rnn(xr, l, conv_w, conv_b, w_g, b_a, b_x, lam, nb, ts_rnn)
        attn = _attention(qkv, l, attn_sink)
        h, hb = _merge(hb, attn, hf, hbk, h, l, w_z, w_pa, w_pr, w_o, g1, b1, alpha, tm_merge)
        out = _ffn(h, hb, l, p, w_u, w_d, ffn_conv_w, ffn_cb, w_pg, w_pl, g2, b2,
                   alpha, tm_ffn, TN, batch_major_out=(l == depth - 1))
        if l < depth - 1:
            h, hb = out
    return out
```

```python
import functools
import math

import jax
import jax.numpy as jnp
from jax import lax
from jax.experimental import pallas as pl
from jax.experimental.pallas import tpu as pltpu

F32 = jnp.float32
BF16 = jnp.bfloat16

N_HEADS = 8
N_KV_HEADS = 2
HEAD_DIM = 128
Q_GROUP = N_HEADS // N_KV_HEADS
WINDOW = 128
ROPE_THETA = 500000.0
ROPE_DIM = HEAD_DIM // 4
N_RNN_BLOCKS = 4
RNN_CONV_W = 4
FFN_CONV_W = 3
RG_C = 8.0
LN_EPS = 1e-5
NEG_INF = -1e30
LOG2E = math.log2(math.e)
GELU_C = math.sqrt(2.0 / math.pi)
GELU_A = 0.044715

LANES = 128
SUBLANES = 8
BF16_ROWS = 16
MXU_COLS = 256
VMEM_LIMIT = 56 * 1024 * 1024

TM_LN = 1024
TM_PROJ = 2048
TM_MERGE = 512
TM_FFN = 512
TS_RNN = 256
TN = 512
TF_FFN = 1536
TR_CAST = 256


def _cparams(*sem):
    return pltpu.CompilerParams(dimension_semantics=sem, vmem_limit_bytes=VMEM_LIMIT)


def _layer_norm(x, g, b):
    mu = jnp.mean(x, axis=-1, keepdims=True)
    xc = x - mu
    var = jnp.mean(xc * xc, axis=-1, keepdims=True)
    return xc * lax.rsqrt(var + LN_EPS) * g + b


def _gelu(x):
    k1 = -2.0 * GELU_C * LOG2E
    return x / (1.0 + jnp.exp2(x * (k1 + (k1 * GELU_A) * (x * x))))


def _softplus(x):
    return jnp.maximum(x, 0.0) + jnp.log1p(jnp.exp(-jnp.abs(x)))


def _const_spec(shape):
    nd = len(shape)
    return pl.BlockSpec(shape, lambda *_: (0,) * nd, pipeline_mode=pl.Buffered(1))


def _layer_spec(arr, l):
    tail = (0,) * (arr.ndim - 1)
    return pl.BlockSpec((None,) + arr.shape[1:], lambda *_: (l,) + tail, pipeline_mode=pl.Buffered(1))


def _slab_scratch(rows, cols):
    return pltpu.VMEM((cols // LANES, rows, LANES), F32)


def _put_sequence(slab_s, b, val):
    tt, n = val.shape
    for c in range(n // LANES):
        slab_s[c, pl.ds(b, tt, stride=SUBLANES), :] = val[:, c * LANES:(c + 1) * LANES]


def _get_sequence(slab_s, b, tt, c):
    return slab_s[c, pl.ds(b, tt, stride=SUBLANES), :]


def _slabs(slab_s):
    return jnp.concatenate([slab_s[c] for c in range(slab_s.shape[0])], axis=1)


def _ln_in_kernel(x_ref, g_ref, b_ref, h_ref, hb_ref, t_s):
    nb = x_ref.shape[0]
    for b in range(nb):
        _put_sequence(t_s, b, _layer_norm(x_ref[b], g_ref[...], b_ref[...]))
    y = _slabs(t_s)
    h_ref[...] = y
    hb_ref[...] = y.astype(BF16)


def _ln_in(x, g, b, tm):
    nb, seq, d = x.shape
    m = nb * seq
    tt = tm // nb
    return pl.pallas_call(
        _ln_in_kernel,
        grid=(m // tm,),
        in_specs=[pl.BlockSpec((nb, tt, d), lambda i: (0, i, 0)),
                  _const_spec((1, d)), _const_spec((1, d))],
        out_specs=[pl.BlockSpec((tm, d), lambda i: (i, 0)),
                   pl.BlockSpec((tm, d), lambda i: (i, 0))],
        out_shape=[jax.ShapeDtypeStruct((m, d), F32), jax.ShapeDtypeStruct((m, d), BF16)],
        scratch_shapes=[_slab_scratch(tm, d)],
        compiler_params=_cparams("parallel"),
        name="ln_in",
    )(x, g.reshape(1, d), b.reshape(1, d))


def _w_in_cast_kernel(w_ref, wa_ref, wz_ref):
    half = ROPE_DIM // 2
    mid = HEAD_DIM // 2
    pieces = ((0, half), (ROPE_DIM, mid + half), (half, ROPE_DIM), (mid + half, HEAD_DIM))
    na = wa_ref.shape[1]
    for hd in range(N_HEADS + N_KV_HEADS):
        x = w_ref[:, hd * HEAD_DIM:(hd + 1) * HEAD_DIM]
        y = jnp.concatenate([x[:, a:b] for a, b in pieces], axis=1)
        wa_ref[:, hd * HEAD_DIM:(hd + 1) * HEAD_DIM] = y.astype(BF16)
    rest = (N_HEADS + N_KV_HEADS) * HEAD_DIM
    wa_ref[:, rest:] = w_ref[:, rest:na].astype(BF16)
    wz_ref[...] = w_ref[:, na:].astype(BF16)


def _w_in_cast(w_in, na, tr):
    depth, d, n = w_in.shape
    return pl.pallas_call(
        _w_in_cast_kernel,
        grid=(depth, d // tr),
        in_specs=[pl.BlockSpec((None, tr, n), lambda l, i: (l, i, 0))],
        out_specs=[pl.BlockSpec((None, tr, na), lambda l, i: (l, i, 0)),
                   pl.BlockSpec((None, tr, n - na), lambda l, i: (l, i, 0))],
        out_shape=[jax.ShapeDtypeStruct((depth, d, na), BF16),
                   jax.ShapeDtypeStruct((depth, d, n - na), BF16)],
        compiler_params=_cparams("parallel", "parallel"),
        name="w_in_cast",
    )(w_in)


def _in_proj_kernel(h_ref, w_ref, c_ref, s_ref, o_ref, xr_ref, *z_scratch, n_qkv_tiles, n_rope_last):
    step = pl.program_id(1)
    j = step // 2
    is_xr = step % 2 == 1
    last = n_qkv_tiles - 1
    nb, tt, _ = o_ref.shape
    h = h_ref[...]

    @pl.when(is_xr)
    def _():
        xr_ref[...] = jnp.dot(h, w_ref[...], preferred_element_type=F32)

    @pl.when(jnp.logical_not(is_xr))
    def _():
        c, s = c_ref[...], s_ref[...]

        def rope(zc):
            return zc * c + pltpu.roll(zc, LANES // 2, 1) * s

        heads_per_dot = MXU_COLS // LANES

        def emit(z_s, ch, cc, rotary):
            for b in range(nb):
                zc = _get_sequence(z_s, b, tt, ch)
                o_ref[b, :, cc * LANES:(cc + 1) * LANES] = (rope(zc) if rotary else zc).astype(BF16)

        for cd, z_s in enumerate(z_scratch):
            z = jnp.dot(h, w_ref[:, cd * MXU_COLS:(cd + 1) * MXU_COLS], preferred_element_type=F32)
            for ch in range(heads_per_dot):
                z_s[ch] = z[:, ch * LANES:(ch + 1) * LANES]
            for ch in range(heads_per_dot):
                cc = cd * heads_per_dot + ch
                if cc < n_rope_last:
                    emit(z_s, ch, cc, True)
                else:
                    pl.when(j < last)(functools.partial(emit, z_s, ch, cc, True))
                    pl.when(j == last)(functools.partial(emit, z_s, ch, cc, False))


def _in_proj(hb, l, w, rope_c, rope_s, nb, n_qkv, tm, tn):
    m, d = hb.shape
    n = w.shape[2]
    seq = m // nb
    tt = tm // nb
    n_rope_cols = (N_HEADS + N_KV_HEADS) * HEAD_DIM
    assert n % tn == 0 and n_qkv % tn == 0 and tn % MXU_COLS == 0 and (n_qkv - tn) <= n_rope_cols <= n_qkv
    nq = n_qkv // tn
    assert (n - n_qkv) // tn == nq - 1
    n_rope_last = (n_rope_cols - (n_qkv - tn)) // LANES
    tab = pl.BlockSpec((tt, LANES), lambda i, j: (i, 0))
    return pl.pallas_call(
        functools.partial(_in_proj_kernel, n_qkv_tiles=nq, n_rope_last=n_rope_last),
        grid=(m // tm, n // tn),
        in_specs=[pl.BlockSpec((tm, d), lambda i, s: (i, 0)),
                  pl.BlockSpec((None, d, tn), lambda i, s: (l, 0, s // 2 + (s % 2) * nq)),
                  tab, tab],
        out_specs=[pl.BlockSpec((nb, tt, tn), lambda i, s: (0, i, s // 2)),
                   pl.BlockSpec((tm, tn), lambda i, s: (i, jnp.maximum(s - 1, 0) // 2))],
        out_shape=[jax.ShapeDtypeStruct((nb, seq, n_qkv), BF16),
                   jax.ShapeDtypeStruct((m, n - n_qkv), F32)],
        scratch_shapes=[_slab_scratch(tm, MXU_COLS)] * (tn // MXU_COLS),
        compiler_params=_cparams("parallel", "arbitrary"),
        name="in_proj",
    )(hb, w, rope_c, rope_s)


def _rnn_kernel(xf_ref, xb_ref, cw_ref, cb_ref, wg_ref, ba_ref, bx_ref, lam_ref,
                hf_ref, hb_ref, xsf_s, xsb_s, af_s, uf_s, ab_s, ub_s, carry_s, *, nb):
    i = pl.program_id(1)
    rows, bw = hf_ref.shape
    ts = rows // nb
    halo = (RNN_CONV_W - 1) * nb

    @pl.when(i == 0)
    def _():
        xsf_s[0:halo, :] = jnp.zeros((halo, bw), F32)
        xsb_s[rows:rows + halo, :] = jnp.zeros((halo, bw), F32)
        carry_s[...] = jnp.zeros_like(carry_s)

    def gates(d, xc, start, a_s, u_s):
        g = jnp.dot(xc.astype(BF16), wg_ref[d, 0], preferred_element_type=F32)
        gate_a = 1.0 / (1.0 + jnp.exp2(g[:, :bw] * (-LOG2E) + ba_ref[d] * (-LOG2E)))
        gate_x = 1.0 / (1.0 + jnp.exp2(g[:, bw:] * (-LOG2E) + bx_ref[d] * (-LOG2E)))
        log_a = gate_a * ((-RG_C) * _softplus(-lam_ref[d]))
        a = jnp.exp(log_a)
        w = jnp.tanh(log_a) * (-1.0 - a * a)
        mult = jnp.where(w > 0.0, w * lax.rsqrt(w), 0.0)
        xg = xc * gate_x
        a_s[...] = a
        u_s[...] = xg * mult
        return xg[start, :]

    xsf_s[halo:halo + rows, :] = xf_ref[...]
    yf = cb_ref[0] + cw_ref[0, 0] * xsf_s[halo:halo + rows, :]
    for k in range(1, RNN_CONV_W):
        yf = yf + cw_ref[0, k] * xsf_s[halo - k * nb:halo - k * nb + rows, :]
    xsf_s[0:halo, :] = xsf_s[rows:rows + halo, :]
    first_f = gates(0, yf, slice(0, nb), af_s, uf_s)

    xsb_s[0:rows, :] = xb_ref[...]
    yb = cb_ref[1] + cw_ref[1, 0] * xsb_s[0:rows, :]
    for k in range(1, RNN_CONV_W):
        yb = yb + cw_ref[1, k] * xsb_s[k * nb:k * nb + rows, :]
    xsb_s[rows:rows + halo, :] = xsb_s[0:halo, :]
    first_b = gates(1, yb, slice(rows - nb, rows), ab_s, ub_s)

    @pl.when(i == 0)
    def _():
        uf_s[0:nb, :] = first_f
        ub_s[rows - nb:rows, :] = first_b

    def step(tt, carry):
        cf, cb = carry
        rf = pl.ds(pl.multiple_of(tt * nb, nb), nb)
        rb = pl.ds(pl.multiple_of((ts - 1 - tt) * nb, nb), nb)
        cf = af_s[rf, :] * cf + uf_s[rf, :]
        cb = ab_s[rb, :] * cb + ub_s[rb, :]
        hf_ref[rf, :] = cf
        hb_ref[rb, :] = cb
        return cf, cb

    cf, cb = lax.fori_loop(0, ts, step, (carry_s[0], carry_s[1]), unroll=8)
    carry_s[0] = cf
    carry_s[1] = cb


def _rnn(xr, l, conv_w, conv_b, wg, b_a, b_x, lam, nb, ts):
    m, d = xr.shape
    bw = d // N_RNN_BLOCKS
    rows = ts * nb
    nt = m // rows
    halo = (RNN_CONV_W - 1) * nb
    xs = pltpu.VMEM((rows + halo, bw), F32)
    au = pltpu.VMEM((rows, bw), F32)
    vec = lambda: pl.BlockSpec((None, 2, 1, bw), lambda c, i: (l, 0, 0, c))
    return pl.pallas_call(
        functools.partial(_rnn_kernel, nb=nb),
        grid=(N_RNN_BLOCKS, nt),
        in_specs=[pl.BlockSpec((rows, bw), lambda c, i: (i, c)),
                  pl.BlockSpec((rows, bw), lambda c, i: (nt - 1 - i, c)),
                  pl.BlockSpec((None, 2, RNN_CONV_W, 1, bw), lambda c, i: (l, 0, 0, 0, c)),
                  vec(),
                  pl.BlockSpec((None, 2, 1, bw, 2 * bw), lambda c, i: (l, 0, c, 0, 0)),
                  vec(), vec(), vec()],
        out_specs=[pl.BlockSpec((rows, bw), lambda c, i: (i, c)),
                   pl.BlockSpec((rows, bw), lambda c, i: (nt - 1 - i, c))],
        out_shape=[jax.ShapeDtypeStruct((m, d), F32)] * 2,
        scratch_shapes=[xs, xs, au, au, au, au, pltpu.VMEM((2, nb, bw), F32)],
        compiler_params=_cparams("parallel", "arbitrary"),
        name="rg_lru",
    )(xr, xr, conv_w, conv_b, wg, b_a, b_x, lam)


def _attn_kernel(sink_ref, q_ref, k_ref, v_ref, o_ref, kp_s, vp_s, s_s, p_s, *, l):
    hk = pl.program_id(1)
    seq = q_ref.shape[1]
    blk = WINDOW
    nblk = seq // blk
    span = 3 * blk
    c1 = HEAD_DIM ** -0.5 * LOG2E

    kp_s[0] = jnp.zeros((HEAD_DIM, blk), BF16)
    for jb in range(nblk):
        kp_s[jb + 1] = k_ref[0, jb * blk:(jb + 1) * blk, :].astype(F32).T.astype(BF16)
    kp_s[nblk + 1] = jnp.zeros((HEAD_DIM, blk), BF16)
    vp_s[0:blk, 0:HEAD_DIM] = jnp.zeros((blk, HEAD_DIM), BF16)
    vp_s[blk:blk + seq, 0:HEAD_DIM] = v_ref[0]
    vp_s[blk + seq:2 * blk + seq, 0:HEAD_DIM] = jnp.zeros((blk, HEAD_DIM), BF16)
    vp_s[:, HEAD_DIM:] = jnp.ones((seq + 2 * blk, HEAD_DIM), BF16)

    qi = lax.broadcasted_iota(jnp.int32, (blk, blk), 0)
    kj = lax.broadcasted_iota(jnp.int32, (blk, blk), 1)
    bias_prev = jnp.where(kj >= qi, 0.0, NEG_INF)
    bias_next = jnp.where(kj <= qi, 0.0, NEG_INF)

    def logits(n, slot):
        q0 = pl.multiple_of(n * blk, blk)
        qs = jnp.concatenate(
            [q_ref[0, pl.ds(q0, blk), g * HEAD_DIM:(g + 1) * HEAD_DIM] for g in range(Q_GROUP)], axis=0)
        kw = jnp.concatenate([kp_s[n], kp_s[n + 1], kp_s[n + 2]], axis=1)
        s_s[slot] = jnp.dot(qs, kw, preferred_element_type=F32)

    def softmax(n, slot):
        bp = jnp.where(n == 0, NEG_INF, bias_prev)
        bn = jnp.where(n == nblk - 1, NEG_INF, bias_next)
        m2s = []
        for g in range(Q_GROUP):
            r = slice(g * blk, (g + 1) * blk)
            s_prev = s_s[slot, r, 0:blk] + bp
            s_cur = s_s[slot, r, blk:2 * blk]
            s_next = s_s[slot, r, 2 * blk:span] + bn
            m_raw = jnp.max(jnp.maximum(jnp.maximum(s_prev, s_cur), s_next), axis=-1, keepdims=True)
            m2 = jnp.maximum(m_raw * c1, sink_ref[l, hk * Q_GROUP + g] * LOG2E)
            p_s[slot, r, 0:blk] = jnp.exp2(s_prev * c1 - m2).astype(BF16)
            p_s[slot, r, blk:2 * blk] = jnp.exp2(s_cur * c1 - m2).astype(BF16)
            p_s[slot, r, 2 * blk:span] = jnp.exp2(s_next * c1 - m2).astype(BF16)
            m2s.append(m2)
        return tuple(m2s)

    def values(n, slot, m2s):
        q0 = n * blk if isinstance(n, int) else pl.multiple_of(n * blk, blk)
        oe = jnp.dot(p_s[slot], vp_s[pl.ds(q0, span), :], preferred_element_type=F32)
        for g in range(Q_GROUP):
            r = slice(g * blk, (g + 1) * blk)
            den = oe[r, HEAD_DIM:] + jnp.exp2(sink_ref[l, hk * Q_GROUP + g] * LOG2E - m2s[g])
            o_ref[0, pl.ds(q0, blk), g * HEAD_DIM:(g + 1) * HEAD_DIM] = (oe[r, :HEAD_DIM] / den).astype(BF16)

    logits(0, 0)
    p_s[1] = jnp.zeros(p_s.shape[1:], BF16)

    def body(i, m_prev):
        n = 2 * i
        logits(n + 1, 1)
        m_even = softmax(n, 0)
        values(jnp.maximum(n - 1, 0), 1, m_prev)
        logits(jnp.minimum(n + 2, nblk - 1), 0)
        m_odd = softmax(n + 1, 1)
        values(n, 0, m_even)
        return m_odd

    assert nblk % 2 == 0
    m_init = tuple(jnp.zeros((blk, 1), F32) for _ in range(Q_GROUP))
    m_last = lax.fori_loop(0, nblk // 2, body, m_init)
    values(nblk - 1, 1, m_last)


def _attention(qkv3, l, sink):
    nb, seq, _ = qkv3.shape
    gw = Q_GROUP * HEAD_DIM
    k_blk0 = N_HEADS
    v_blk0 = N_HEADS + N_KV_HEADS
    rows = Q_GROUP * WINDOW
    return pl.pallas_call(
        functools.partial(_attn_kernel, l=l),
        grid=(nb, N_KV_HEADS),
        in_specs=[pl.BlockSpec(memory_space=pltpu.SMEM),
                  pl.BlockSpec((1, seq, gw), lambda b, h: (b, 0, h)),
                  pl.BlockSpec((1, seq, HEAD_DIM), lambda b, h: (b, 0, k_blk0 + h)),
                  pl.BlockSpec((1, seq, HEAD_DIM), lambda b, h: (b, 0, v_blk0 + h))],
        out_specs=pl.BlockSpec((1, seq, gw), lambda b, h: (b, 0, h)),
        out_shape=jax.ShapeDtypeStruct((nb, seq, N_HEADS * HEAD_DIM), BF16),
        scratch_shapes=[pltpu.VMEM((seq // WINDOW + 2, HEAD_DIM, WINDOW), BF16),
                        pltpu.VMEM((seq + 2 * WINDOW, 2 * HEAD_DIM), BF16),
                        pltpu.VMEM((2, rows, 3 * WINDOW), F32),
                        pltpu.VMEM((2, rows, 3 * WINDOW), BF16)],
        compiler_params=_cparams("parallel", "parallel"),
        name="swa_attention",
    )(sink, qkv3, qkv3, qkv3)


def _merge_kernel(hb_ref, attn_ref, hf_ref, hbk_ref, h_ref,
                  wz_ref, wpa_ref, wpr_ref, wo_ref, g_ref, b_ref, o_ref, ob_ref, at_s, *, alpha):
    d = h_ref.shape[1]
    nb = attn_ref.shape[0]
    z = jnp.dot(hb_ref[...], wz_ref[...], preferred_element_type=F32)
    rnn = (hf_ref[...] + hbk_ref[...]) * _gelu(z[:, :d])
    for b in range(nb):
        _put_sequence(at_s, b, attn_ref[b].astype(F32))
    pa = jnp.dot(_slabs(at_s).astype(BF16), wpa_ref[...], preferred_element_type=F32)
    pr = jnp.dot(rnn.astype(BF16), wpr_ref[...], preferred_element_type=F32)
    merged = jax.nn.sigmoid(z[:, d:2 * d]) * pa + jax.nn.sigmoid(z[:, 2 * d:]) * pr
    o = jnp.dot(merged.astype(BF16), wo_ref[...], preferred_element_type=F32)
    y = _layer_norm(alpha * h_ref[...] + o, g_ref[...], b_ref[...])
    o_ref[...] = y
    ob_ref[...] = y.astype(BF16)


def _merge(hb, attn3, hf, hbk, h, l, wz, wpa, wpr, wo, g, b, alpha, tm):
    m, d = h.shape
    nb = attn3.shape[0]
    row = pl.BlockSpec((tm, d), lambda i: (i, 0))
    return pl.pallas_call(
        functools.partial(_merge_kernel, alpha=alpha),
        grid=(m // tm,),
        in_specs=[row, pl.BlockSpec((nb, tm // nb, d), lambda i: (0, i, 0)), row, row, row]
                 + [_layer_spec(a, l) for a in (wz, wpa, wpr, wo, g, b)],
        out_specs=[row, row],
        out_shape=[jax.ShapeDtypeStruct((m, d), F32), jax.ShapeDtypeStruct((m, d), BF16)],
        scratch_shapes=[_slab_scratch(tm, d)],
        compiler_params=_cparams("parallel"),
        name="merge_out_ln",
    )(hb, attn3, hf, hbk, h, wz, wpa, wpr, wo, g, b)


def _ffn_kernel(hb_ref, hprev_ref, hnext_ref, h_ref, p_ref, wu_ref, wd_ref, cw_ref, cb_ref,
                wpg_ref, wple_ref, g_ref, b_ref, *rest, alpha, tf, nb, batch_major_out):
    if batch_major_out:
        o_ref, p_s, y_s = rest
    else:
        o_ref, ob_ref, p_s = rest
    i = pl.program_id(0)
    tm = hb_ref.shape[0]
    dff = wd_ref.shape[0]
    halo = hprev_ref.shape[0]
    hb = hb_ref[...]
    hp = hprev_ref[...]
    hn = hnext_ref[...]
    hp = jnp.where(i == 0, jnp.zeros_like(hp), hp)
    hn = jnp.where(i == pl.num_programs(0) - 1, jnp.zeros_like(hn), hn)
    lhs = jnp.concatenate([hp, hb, hn], axis=0)
    acc = None
    for j in range(dff // tf):
        cs = slice(j * tf, (j + 1) * tf)
        gate = jnp.dot(lhs, wu_ref[:, cs], preferred_element_type=F32)
        val = jnp.dot(hb, wu_ref[:, dff + j * tf:dff + (j + 1) * tf], preferred_element_type=F32)
        conv = (cw_ref[0:1, cs] * gate[halo - nb:halo - nb + tm] + cw_ref[1:2, cs] * gate[halo:halo + tm]
                + cw_ref[2:3, cs] * gate[halo + nb:halo + nb + tm] + cb_ref[:, cs])
        act = (_gelu(conv) * val).astype(BF16)
        pc = jnp.dot(act, wd_ref[cs, :], preferred_element_type=F32)
        acc = pc if acc is None else acc + pc

    for b in range(nb):
        _put_sequence(p_s, b, p_ref[b])
    ple = (jax.nn.sigmoid(jnp.dot(hb, wpg_ref[...], preferred_element_type=F32))
           * jnp.dot(_slabs(p_s).astype(BF16), wple_ref[...], preferred_element_type=F32))
    y = _layer_norm(alpha * h_ref[...] + acc + ple, g_ref[...], b_ref[...])
    if batch_major_out:
        for c in range(y_s.shape[0]):
            y_s[c] = y[:, c * LANES:(c + 1) * LANES]
        for b in range(nb):
            for c in range(y_s.shape[0]):
                o_ref[b, :, c * LANES:(c + 1) * LANES] = _get_sequence(y_s, b, tm // nb, c)
    else:
        o_ref[...] = y
        ob_ref[...] = y.astype(BF16)


def _ffn(h, hb, l, p4, w_up, w_down, cw, cb, wpg, wple, g, b, alpha, tm, tf, batch_major_out):
    m, d = h.shape
    _, nb, seq, dple = p4.shape
    tt = tm // nb
    halo = BF16_ROWS
    per = tm // halo
    last = m // halo - 1
    row = pl.BlockSpec((tm, d), lambda i: (i, 0))
    if batch_major_out:
        out_specs = pl.BlockSpec((nb, tt, d), lambda i: (0, i, 0))
        out_shape = jax.ShapeDtypeStruct((nb, seq, d), F32)
        scratch = [_slab_scratch(tm, dple), _slab_scratch(tm, d)]
    else:
        out_specs = [row, row]
        out_shape = [jax.ShapeDtypeStruct((m, d), F32), jax.ShapeDtypeStruct((m, d), BF16)]
        scratch = [_slab_scratch(tm, dple)]
    return pl.pallas_call(
        functools.partial(_ffn_kernel, alpha=alpha, tf=tf, nb=nb, batch_major_out=batch_major_out),
        grid=(m // tm,),
        in_specs=[row,
                  pl.BlockSpec((halo, d), lambda i: (jnp.maximum(i * per - 1, 0), 0)),
                  pl.BlockSpec((halo, d), lambda i: (jnp.minimum((i + 1) * per, last), 0)),
                  row,
                  pl.BlockSpec((None, nb, tt, dple), lambda i: (l, 0, i, 0))]
                 + [_layer_spec(a, l) for a in (w_up, w_down, cw, cb, wpg, wple, g, b)],
        out_specs=out_specs,
        out_shape=out_shape,
        scratch_shapes=scratch,
        compiler_params=_cparams("parallel"),
        name="ffn_ple_ln",
    )(hb, hb, hb, h, p4, w_up, w_down, cw, cb, wpg, wple, g, b)


def _rope_tables(seq):
    half = ROPE_DIM // 2
    mid = HEAD_DIM // 2
    pos = jnp.arange(seq, dtype=F32)
    inv = ROPE_THETA ** (-jnp.arange(0, ROPE_DIM, 2, dtype=F32) / ROPE_DIM)
    ang = pos[:, None] * inv[None, :]
    cos, sin = jnp.cos(ang), jnp.sin(ang)
    ones = jnp.ones((seq, mid - half), F32)
    zeros = jnp.zeros((seq, mid - half), F32)
    c = jnp.concatenate([cos, ones, cos, ones], axis=1)
    s = jnp.concatenate([-sin, zeros, sin, zeros], axis=1)
    return c, s


def kernel(x, p, ln_in_g, ln_in_b, w_in, attn_sink, rnn_conv_w, rnn_conv_b, rg_w_a, rg_b_a, rg_w_x, rg_b_x, rg_lambda, w_proj_attn, w_proj_rnn, w_out, ln1_g, ln1_b, w_up, ffn_conv_w, ffn_conv_b, w_down, w_ple, w_ple_gate, ln2_g, ln2_b):
    nb, seq, d = x.shape
    depth = w_in.shape[0]
    m = nb * seq
    alpha = float((2 * depth) ** 0.25)
    n_qkv = (N_HEADS + 2 * N_KV_HEADS) * HEAD_DIM
    assert nb == SUBLANES and d == N_HEADS * HEAD_DIM and seq % WINDOW == 0 and seq >= 3 * WINDOW

    tm_ln = min(TM_LN, m)
    tm_proj = min(TM_PROJ, m)
    tm_merge = min(TM_MERGE, m)
    tm_ffn = min(TM_FFN, m)
    ts_rnn = min(TS_RNN, seq)

    w_a, w_z = _w_in_cast(w_in, n_qkv + d, min(TR_CAST, d))
    w_g = jnp.concatenate([rg_w_a, rg_w_x], axis=-1).astype(BF16)
    w_pa, w_pr, w_o = w_proj_attn.astype(BF16), w_proj_rnn.astype(BF16), w_out.astype(BF16)
    w_u, w_d = w_up.astype(BF16), w_down.astype(BF16)
    w_pg, w_pl = w_ple_gate.astype(BF16), w_ple.astype(BF16)

    row = lambda a: a.reshape(a.shape[:-1] + (1, a.shape[-1]))
    conv_w, conv_b = row(rnn_conv_w), row(rnn_conv_b)
    b_a, b_x, lam = row(rg_b_a), row(rg_b_x), row(rg_lambda)
    g1, b1, g2, b2, ffn_cb = row(ln1_g), row(ln1_b), row(ln2_g), row(ln2_b), row(ffn_conv_b)

    rope_c, rope_s = _rope_tables(seq)
    h, hb = _ln_in(x, ln_in_g, ln_in_b, tm_ln)

    for l in range(depth):
        qkv, xr = _in_proj(hb, l, w_a, rope_c, rope_s, nb, n_qkv, tm_proj, TN)
        hf, hbk = _rnn(xr, l, conv_w, conv_b, w_g, b_a, b_x, lam, nb, ts_rnn)
        attn = _attention(qkv, l, attn_sink)
        h, hb = _merge(hb, attn, hf, hbk, h, l, w_z, w_pa, w_pr, w_o, g1, b1, alpha, tm_merge)
        out = _ffn(h, hb, l, p, w_u, w_d, ffn_conv_w, ffn_cb, w_pg, w_pl, g2, b2,
                   alpha, tm_ffn, TF_FFN, batch_major_out=(l == depth - 1))
        if l < depth - 1:
            h, hb = out
    return out
```

```python
import functools
import math

import jax
import jax.numpy as jnp
from jax import lax
from jax.experimental import pallas as pl
from jax.experimental.pallas import tpu as pltpu

F32 = jnp.float32
BF16 = jnp.bfloat16

N_HEADS = 8
N_KV_HEADS = 2
HEAD_DIM = 128
Q_GROUP = N_HEADS // N_KV_HEADS
WINDOW = 128
ROPE_THETA = 500000.0
ROPE_DIM = HEAD_DIM // 4
N_RNN_BLOCKS = 4
RNN_CONV_W = 4
FFN_CONV_W = 3
RG_C = 8.0
LN_EPS = 1e-5
NEG_INF = -1e30
LOG2E = math.log2(math.e)
GELU_C = math.sqrt(2.0 / math.pi)
GELU_A = 0.044715

LANES = 128
SUBLANES = 8
BF16_ROWS = 16
MXU_COLS = 256
VMEM_LIMIT = 56 * 1024 * 1024

TM_LN = 1024
TM_PROJ = 2048
TM_MERGE = 512
TM_FFN = 512
TS_RNN = 128
RNN_BLOCKS_PER_STEP = 2
TN = 512
TF_FFN = 1536
TR_CAST = 256


def _cparams(*sem):
    return pltpu.CompilerParams(dimension_semantics=sem, vmem_limit_bytes=VMEM_LIMIT)


def _layer_norm(x, g, b):
    mu = jnp.mean(x, axis=-1, keepdims=True)
    xc = x - mu
    var = jnp.mean(xc * xc, axis=-1, keepdims=True)
    return xc * lax.rsqrt(var + LN_EPS) * g + b


def _gelu(x):
    k1 = -2.0 * GELU_C * LOG2E
    return x / (1.0 + jnp.exp2(x * (k1 + (k1 * GELU_A) * (x * x))))


def _softplus(x):
    return jnp.maximum(x, 0.0) + jnp.log1p(jnp.exp(-jnp.abs(x)))


def _const_spec(shape):
    nd = len(shape)
    return pl.BlockSpec(shape, lambda *_: (0,) * nd, pipeline_mode=pl.Buffered(1))


def _layer_spec(arr, l):
    tail = (0,) * (arr.ndim - 1)
    return pl.BlockSpec((None,) + arr.shape[1:], lambda *_: (l,) + tail, pipeline_mode=pl.Buffered(1))


def _slab_scratch(rows, cols):
    return pltpu.VMEM((cols // LANES, rows, LANES), F32)


def _put_sequence(slab_s, b, val):
    tt, n = val.shape
    for c in range(n // LANES):
        slab_s[c, pl.ds(b, tt, stride=SUBLANES), :] = val[:, c * LANES:(c + 1) * LANES]


def _get_sequence(slab_s, b, tt, c):
    return slab_s[c, pl.ds(b, tt, stride=SUBLANES), :]


def _slabs(slab_s):
    return jnp.concatenate([slab_s[c] for c in range(slab_s.shape[0])], axis=1)


def _ln_in_kernel(x_ref, g_ref, b_ref, h_ref, hb_ref, t_s):
    nb = x_ref.shape[0]
    for b in range(nb):
        _put_sequence(t_s, b, _layer_norm(x_ref[b], g_ref[...], b_ref[...]))
    y = _slabs(t_s)
    h_ref[...] = y
    hb_ref[...] = y.astype(BF16)


def _ln_in(x, g, b, tm):
    nb, seq, d = x.shape
    m = nb * seq
    tt = tm // nb
    return pl.pallas_call(
        _ln_in_kernel,
        grid=(m // tm,),
        in_specs=[pl.BlockSpec((nb, tt, d), lambda i: (0, i, 0)),
                  _const_spec((1, d)), _const_spec((1, d))],
        out_specs=[pl.BlockSpec((tm, d), lambda i: (i, 0)),
                   pl.BlockSpec((tm, d), lambda i: (i, 0))],
        out_shape=[jax.ShapeDtypeStruct((m, d), F32), jax.ShapeDtypeStruct((m, d), BF16)],
        scratch_shapes=[_slab_scratch(tm, d)],
        compiler_params=_cparams("parallel"),
        name="ln_in",
    )(x, g.reshape(1, d), b.reshape(1, d))


def _w_in_cast_kernel(w_ref, wa_ref, wz_ref):
    half = ROPE_DIM // 2
    mid = HEAD_DIM // 2
    pieces = ((0, half), (ROPE_DIM, mid + half), (half, ROPE_DIM), (mid + half, HEAD_DIM))
    na = wa_ref.shape[1]
    for hd in range(N_HEADS + N_KV_HEADS):
        x = w_ref[:, hd * HEAD_DIM:(hd + 1) * HEAD_DIM]
        y = jnp.concatenate([x[:, a:b] for a, b in pieces], axis=1)
        wa_ref[:, hd * HEAD_DIM:(hd + 1) * HEAD_DIM] = y.astype(BF16)
    rest = (N_HEADS + N_KV_HEADS) * HEAD_DIM
    wa_ref[:, rest:] = w_ref[:, rest:na].astype(BF16)
    wz_ref[...] = w_ref[:, na:].astype(BF16)


def _w_in_cast(w_in, na, tr):
    depth, d, n = w_in.shape
    return pl.pallas_call(
        _w_in_cast_kernel,
        grid=(depth, d // tr),
        in_specs=[pl.BlockSpec((None, tr, n), lambda l, i: (l, i, 0))],
        out_specs=[pl.BlockSpec((None, tr, na), lambda l, i: (l, i, 0)),
                   pl.BlockSpec((None, tr, n - na), lambda l, i: (l, i, 0))],
        out_shape=[jax.ShapeDtypeStruct((depth, d, na), BF16),
                   jax.ShapeDtypeStruct((depth, d, n - na), BF16)],
        compiler_params=_cparams("parallel", "parallel"),
        name="w_in_cast",
    )(w_in)


def _in_proj_kernel(h_ref, w_ref, c_ref, s_ref, o_ref, xr_ref, *z_scratch, n_qkv_tiles, n_rope_last):
    step = pl.program_id(1)
    j = step // 2
    is_xr = step % 2 == 1
    last = n_qkv_tiles - 1
    nb, tt, _ = o_ref.shape
    h = h_ref[...]

    @pl.when(is_xr)
    def _():
        xr_ref[...] = jnp.dot(h, w_ref[...], preferred_element_type=F32)

    @pl.when(jnp.logical_not(is_xr))
    def _():
        c, s = c_ref[...], s_ref[...]

        def rope(zc):
            return zc * c + pltpu.roll(zc, LANES // 2, 1) * s

        heads_per_dot = MXU_COLS // LANES

        def emit(z_s, ch, cc, rotary):
            for b in range(nb):
                zc = _get_sequence(z_s, b, tt, ch)
                o_ref[b, :, cc * LANES:(cc + 1) * LANES] = (rope(zc) if rotary else zc).astype(BF16)

        for cd, z_s in enumerate(z_scratch):
            z = jnp.dot(h, w_ref[:, cd * MXU_COLS:(cd + 1) * MXU_COLS], preferred_element_type=F32)
            for ch in range(heads_per_dot):
                z_s[ch] = z[:, ch * LANES:(ch + 1) * LANES]
            for ch in range(heads_per_dot):
                cc = cd * heads_per_dot + ch
                if cc < n_rope_last:
                    emit(z_s, ch, cc, True)
                else:
                    pl.when(j < last)(functools.partial(emit, z_s, ch, cc, True))
                    pl.when(j == last)(functools.partial(emit, z_s, ch, cc, False))


def _in_proj(hb, l, w, rope_c, rope_s, nb, n_qkv, tm, tn):
    m, d = hb.shape
    n = w.shape[2]
    seq = m // nb
    tt = tm // nb
    n_rope_cols = (N_HEADS + N_KV_HEADS) * HEAD_DIM
    assert n % tn == 0 and n_qkv % tn == 0 and tn % MXU_COLS == 0 and (n_qkv - tn) <= n_rope_cols <= n_qkv
    nq = n_qkv // tn
    assert (n - n_qkv) // tn == nq - 1
    n_rope_last = (n_rope_cols - (n_qkv - tn)) // LANES
    tab = pl.BlockSpec((tt, LANES), lambda i, j: (i, 0))
    return pl.pallas_call(
        functools.partial(_in_proj_kernel, n_qkv_tiles=nq, n_rope_last=n_rope_last),
        grid=(m // tm, n // tn),
        in_specs=[pl.BlockSpec((tm, d), lambda i, s: (i, 0)),
                  pl.BlockSpec((None, d, tn), lambda i, s: (l, 0, s // 2 + (s % 2) * nq)),
                  tab, tab],
        out_specs=[pl.BlockSpec((nb, tt, tn), lambda i, s: (0, i, s // 2)),
                   pl.BlockSpec((tm, tn), lambda i, s: (i, jnp.maximum(s - 1, 0) // 2))],
        out_shape=[jax.ShapeDtypeStruct((nb, seq, n_qkv), BF16),
                   jax.ShapeDtypeStruct((m, n - n_qkv), F32)],
        scratch_shapes=[_slab_scratch(tm, MXU_COLS)] * (tn // MXU_COLS),
        compiler_params=_cparams("parallel", "arbitrary"),
        name="in_proj",
    )(hb, w, rope_c, rope_s)


def _rnn_kernel(xf_ref, xb_ref, cw_ref, cb_ref, wg_ref, ba_ref, bx_ref, lam_ref,
                hf_ref, hb_ref, xsf_s, xsb_s, af_s, uf_s, ab_s, ub_s, carry_s, *, nb):
    i = pl.program_id(1)
    rows, bw = hf_ref.shape
    blk = wg_ref.shape[2]
    ts = rows // nb
    halo = (RNN_CONV_W - 1) * nb

    @pl.when(i == 0)
    def _():
        xsf_s[0:halo, :] = jnp.zeros((halo, bw), F32)
        xsb_s[rows:rows + halo, :] = jnp.zeros((halo, bw), F32)
        carry_s[...] = jnp.zeros_like(carry_s)

    def gates(d, half_x, start, a_s, u_s):
        xcb = half_x.astype(BF16)
        gs = [jnp.dot(xcb[:, sb * blk:(sb + 1) * blk], wg_ref[d, sb], preferred_element_type=F32)
              for sb in range(bw // blk)]
        g_a = jnp.concatenate([g[:, :blk] for g in gs], axis=1)
        g_x = jnp.concatenate([g[:, blk:] for g in gs], axis=1)
        t_a = jnp.tanh(g_a + ba_ref[d] * 0.5)
        t_x = jnp.tanh(g_x + bx_ref[d] * 0.5)
        half_c = (-0.5 * RG_C) * _softplus(-lam_ref[d])
        log_a = half_c + half_c * t_a
        a = jnp.exp(log_a)
        w = jnp.tanh(log_a) * (-1.0 - a * a)
        mult = jnp.where(w > 0.0, w * lax.rsqrt(w), 0.0)
        xg = half_x + half_x * t_x
        a_s[...] = a
        u_s[...] = xg * mult
        return xg[start, :]

    xsf_s[halo:halo + rows, :] = xf_ref[...]
    yf = cb_ref[0] * 0.5 + (cw_ref[0, 0] * 0.5) * xsf_s[halo:halo + rows, :]
    for k in range(1, RNN_CONV_W):
        yf = yf + (cw_ref[0, k] * 0.5) * xsf_s[halo - k * nb:halo - k * nb + rows, :]
    xsf_s[0:halo, :] = xsf_s[rows:rows + halo, :]
    first_f = gates(0, yf, slice(0, nb), af_s, uf_s)

    xsb_s[0:rows, :] = xb_ref[...]
    yb = cb_ref[1] * 0.5 + (cw_ref[1, 0] * 0.5) * xsb_s[0:rows, :]
    for k in range(1, RNN_CONV_W):
        yb = yb + (cw_ref[1, k] * 0.5) * xsb_s[k * nb:k * nb + rows, :]
    xsb_s[rows:rows + halo, :] = xsb_s[0:halo, :]
    first_b = gates(1, yb, slice(rows - nb, rows), ab_s, ub_s)

    @pl.when(i == 0)
    def _():
        uf_s[0:nb, :] = first_f
        ub_s[rows - nb:rows, :] = first_b

    def step(tt, carry):
        cf, cb = carry
        rf = pl.ds(pl.multiple_of(tt * nb, nb), nb)
        rb = pl.ds(pl.multiple_of((ts - 1 - tt) * nb, nb), nb)
        cf = af_s[rf, :] * cf + uf_s[rf, :]
        cb = ab_s[rb, :] * cb + ub_s[rb, :]
        hf_ref[rf, :] = cf
        hb_ref[rb, :] = cb
        return cf, cb

    cf, cb = lax.fori_loop(0, ts, step, (carry_s[0], carry_s[1]), unroll=8)
    carry_s[0] = cf
    carry_s[1] = cb


def _rnn(xr, l, conv_w, conv_b, wg, b_a, b_x, lam, nb, ts):
    m, d = xr.shape
    blk = d // N_RNN_BLOCKS
    bw = RNN_BLOCKS_PER_STEP * blk
    rows = ts * nb
    nt = m // rows
    halo = (RNN_CONV_W - 1) * nb
    xs = pltpu.VMEM((rows + halo, bw), F32)
    au = pltpu.VMEM((rows, bw), F32)
    vec = lambda: pl.BlockSpec((None, 2, 1, bw), lambda c, i: (l, 0, 0, c))
    return pl.pallas_call(
        functools.partial(_rnn_kernel, nb=nb),
        grid=(d // bw, nt),
        in_specs=[pl.BlockSpec((rows, bw), lambda c, i: (i, c)),
                  pl.BlockSpec((rows, bw), lambda c, i: (nt - 1 - i, c)),
                  pl.BlockSpec((None, 2, RNN_CONV_W, 1, bw), lambda c, i: (l, 0, 0, 0, c)),
                  vec(),
                  pl.BlockSpec((None, 2, RNN_BLOCKS_PER_STEP, blk, 2 * blk), lambda c, i: (l, 0, c, 0, 0)),
                  vec(), vec(), vec()],
        out_specs=[pl.BlockSpec((rows, bw), lambda c, i: (i, c)),
                   pl.BlockSpec((rows, bw), lambda c, i: (nt - 1 - i, c))],
        out_shape=[jax.ShapeDtypeStruct((m, d), F32)] * 2,
        scratch_shapes=[xs, xs, au, au, au, au, pltpu.VMEM((2, nb, bw), F32)],
        compiler_params=_cparams("parallel", "arbitrary"),
        name="rg_lru",
    )(xr, xr, conv_w, conv_b, wg, b_a, b_x, lam)


def _attn_kernel(sink_ref, q_ref, k_ref, v_ref, o_ref, kp_s, vp_s, s_s, p_s, *, l):
    hk = pl.program_id(1)
    seq = q_ref.shape[1]
    blk = WINDOW
    nblk = seq // blk
    span = 3 * blk
    c1 = HEAD_DIM ** -0.5 * LOG2E

    kp_s[0] = jnp.zeros((HEAD_DIM, blk), BF16)
    for jb in range(nblk):
        kp_s[jb + 1] = k_ref[0, jb * blk:(jb + 1) * blk, :].astype(F32).T.astype(BF16)
    kp_s[nblk + 1] = jnp.zeros((HEAD_DIM, blk), BF16)
    vp_s[0:blk, 0:HEAD_DIM] = jnp.zeros((blk, HEAD_DIM), BF16)
    vp_s[blk:blk + seq, 0:HEAD_DIM] = v_ref[0]
    vp_s[blk + seq:2 * blk + seq, 0:HEAD_DIM] = jnp.zeros((blk, HEAD_DIM), BF16)
    vp_s[:, HEAD_DIM:] = jnp.ones((seq + 2 * blk, HEAD_DIM), BF16)

    qi = lax.broadcasted_iota(jnp.int32, (blk, blk), 0)
    kj = lax.broadcasted_iota(jnp.int32, (blk, blk), 1)
    bias_prev = jnp.where(kj >= qi, 0.0, NEG_INF)
    bias_next = jnp.where(kj <= qi, 0.0, NEG_INF)

    def logits(n, slot):
        q0 = pl.multiple_of(n * blk, blk)
        qs = jnp.concatenate(
            [q_ref[0, pl.ds(q0, blk), g * HEAD_DIM:(g + 1) * HEAD_DIM] for g in range(Q_GROUP)], axis=0)
        kw = jnp.concatenate([kp_s[n], kp_s[n + 1], kp_s[n + 2]], axis=1)
        s_s[slot] = jnp.dot(qs, kw, preferred_element_type=F32)

    def softmax(n, slot):
        bp = jnp.where(n == 0, NEG_INF, bias_prev)
        bn = jnp.where(n == nblk - 1, NEG_INF, bias_next)
        m2s = []
        for g in range(Q_GROUP):
            r = slice(g * blk, (g + 1) * blk)
            s_prev = s_s[slot, r, 0:blk] + bp
            s_cur = s_s[slot, r, blk:2 * blk]
            s_next = s_s[slot, r, 2 * blk:span] + bn
            m_raw = jnp.max(jnp.maximum(jnp.maximum(s_prev, s_cur), s_next), axis=-1, keepdims=True)
            m2 = jnp.maximum(m_raw * c1, sink_ref[l, hk * Q_GROUP + g] * LOG2E)
            p_s[slot, r, 0:blk] = jnp.exp2(s_prev * c1 - m2).astype(BF16)
            p_s[slot, r, blk:2 * blk] = jnp.exp2(s_cur * c1 - m2).astype(BF16)
            p_s[slot, r, 2 * blk:span] = jnp.exp2(s_next * c1 - m2).astype(BF16)
            m2s.append(m2)
        return tuple(m2s)

    def values(n, slot, m2s):
        q0 = n * blk if isinstance(n, int) else pl.multiple_of(n * blk, blk)
        oe = jnp.dot(p_s[slot], vp_s[pl.ds(q0, span), :], preferred_element_type=F32)
        for g in range(Q_GROUP):
            r = slice(g * blk, (g + 1) * blk)
            den = oe[r, HEAD_DIM:] + jnp.exp2(sink_ref[l, hk * Q_GROUP + g] * LOG2E - m2s[g])
            o_ref[0, pl.ds(q0, blk), g * HEAD_DIM:(g + 1) * HEAD_DIM] = (oe[r, :HEAD_DIM] / den).astype(BF16)

    logits(0, 0)
    p_s[1] = jnp.zeros(p_s.shape[1:], BF16)

    def body(i, m_prev):
        n = 2 * i
        logits(n + 1, 1)
        m_even = softmax(n, 0)
        values(jnp.maximum(n - 1, 0), 1, m_prev)
        logits(jnp.minimum(n + 2, nblk - 1), 0)
        m_odd = softmax(n + 1, 1)
        values(n, 0, m_even)
        return m_odd

    assert nblk % 2 == 0
    m_init = tuple(jnp.zeros((blk, 1), F32) for _ in range(Q_GROUP))
    m_last = lax.fori_loop(0, nblk // 2, body, m_init)
    values(nblk - 1, 1, m_last)


def _attention(qkv3, l, sink):
    nb, seq, _ = qkv3.shape
    gw = Q_GROUP * HEAD_DIM
    k_blk0 = N_HEADS
    v_blk0 = N_HEADS + N_KV_HEADS
    rows = Q_GROUP * WINDOW
    return pl.pallas_call(
        functools.partial(_attn_kernel, l=l),
        grid=(nb, N_KV_HEADS),
        in_specs=[pl.BlockSpec(memory_space=pltpu.SMEM),
                  pl.BlockSpec((1, seq, gw), lambda b, h: (b, 0, h)),
                  pl.BlockSpec((1, seq, HEAD_DIM), lambda b, h: (b, 0, k_blk0 + h)),
                  pl.BlockSpec((1, seq, HEAD_DIM), lambda b, h: (b, 0, v_blk0 + h))],
        out_specs=pl.BlockSpec((1, seq, gw), lambda b, h: (b, 0, h)),
        out_shape=jax.ShapeDtypeStruct((nb, seq, N_HEADS * HEAD_DIM), BF16),
        scratch_shapes=[pltpu.VMEM((seq // WINDOW + 2, HEAD_DIM, WINDOW), BF16),
                        pltpu.VMEM((seq + 2 * WINDOW, 2 * HEAD_DIM), BF16),
                        pltpu.VMEM((2, rows, 3 * WINDOW), F32),
                        pltpu.VMEM((2, rows, 3 * WINDOW), BF16)],
        compiler_params=_cparams("parallel", "parallel"),
        name="swa_attention",
    )(sink, qkv3, qkv3, qkv3)


def _merge_kernel(hb_ref, attn_ref, hf_ref, hbk_ref, h_ref,
                  wz_ref, wpa_ref, wpr_ref, wo_ref, g_ref, b_ref, o_ref, ob_ref, at_s, *, alpha):
    d = h_ref.shape[1]
    nb = attn_ref.shape[0]
    z = jnp.dot(hb_ref[...], wz_ref[...], preferred_element_type=F32)
    rnn = (hf_ref[...] + hbk_ref[...]) * _gelu(z[:, :d])
    for b in range(nb):
        _put_sequence(at_s, b, attn_ref[b].astype(F32))
    pa = jnp.dot(_slabs(at_s).astype(BF16), wpa_ref[...], preferred_element_type=F32)
    pr = jnp.dot(rnn.astype(BF16), wpr_ref[...], preferred_element_type=F32)
    merged = jax.nn.sigmoid(z[:, d:2 * d]) * pa + jax.nn.sigmoid(z[:, 2 * d:]) * pr
    o = jnp.dot(merged.astype(BF16), wo_ref[...], preferred_element_type=F32)
    y = _layer_norm(alpha * h_ref[...] + o, g_ref[...], b_ref[...])
    o_ref[...] = y
    ob_ref[...] = y.astype(BF16)


def _merge(hb, attn3, hf, hbk, h, l, wz, wpa, wpr, wo, g, b, alpha, tm):
    m, d = h.shape
    nb = attn3.shape[0]
    row = pl.BlockSpec((tm, d), lambda i: (i, 0))
    return pl.pallas_call(
        functools.partial(_merge_kernel, alpha=alpha),
        grid=(m // tm,),
        in_specs=[row, pl.BlockSpec((nb, tm // nb, d), lambda i: (0, i, 0)), row, row, row]
                 + [_layer_spec(a, l) for a in (wz, wpa, wpr, wo, g, b)],
        out_specs=[row, row],
        out_shape=[jax.ShapeDtypeStruct((m, d), F32), jax.ShapeDtypeStruct((m, d), BF16)],
        scratch_shapes=[_slab_scratch(tm, d)],
        compiler_params=_cparams("parallel"),
        name="merge_out_ln",
    )(hb, attn3, hf, hbk, h, wz, wpa, wpr, wo, g, b)


def _ffn_kernel(hb_ref, hprev_ref, hnext_ref, h_ref, p_ref, wu_ref, wd_ref, cw_ref, cb_ref,
                wpg_ref, wple_ref, g_ref, b_ref, *rest, alpha, tf, nb, batch_major_out):
    if batch_major_out:
        o_ref, p_s, y_s = rest
    else:
        o_ref, ob_ref, p_s = rest
    i = pl.program_id(0)
    tm = hb_ref.shape[0]
    dff = wd_ref.shape[0]
    halo = hprev_ref.shape[0]
    hb = hb_ref[...]
    hp = hprev_ref[...]
    hn = hnext_ref[...]
    hp = jnp.where(i == 0, jnp.zeros_like(hp), hp)
    hn = jnp.where(i == pl.num_programs(0) - 1, jnp.zeros_like(hn), hn)
    lhs = jnp.concatenate([hp, hb, hn], axis=0)
    acc = None
    for j in range(dff // tf):
        cs = slice(j * tf, (j + 1) * tf)
        gate = jnp.dot(lhs, wu_ref[:, cs], preferred_element_type=F32)
        val = jnp.dot(hb, wu_ref[:, dff + j * tf:dff + (j + 1) * tf], preferred_element_type=F32)
        conv = (cw_ref[0:1, cs] * gate[halo - nb:halo - nb + tm] + cw_ref[1:2, cs] * gate[halo:halo + tm]
                + cw_ref[2:3, cs] * gate[halo + nb:halo + nb + tm] + cb_ref[:, cs])
        act = (_gelu(conv) * val).astype(BF16)
        pc = jnp.dot(act, wd_ref[cs, :], preferred_element_type=F32)
        acc = pc if acc is None else acc + pc

    for b in range(nb):
        _put_sequence(p_s, b, p_ref[b])
    ple = (jax.nn.sigmoid(jnp.dot(hb, wpg_ref[...], preferred_element_type=F32))
           * jnp.dot(_slabs(p_s).astype(BF16), wple_ref[...], preferred_element_type=F32))
    y = _layer_norm(alpha * h_ref[...] + acc + ple, g_ref[...], b_ref[...])
    if batch_major_out:
        for c in range(y_s.shape[0]):
            y_s[c] = y[:, c * LANES:(c + 1) * LANES]
        for b in range(nb):
            for c in range(y_s.shape[0]):
                o_ref[b, :, c * LANES:(c + 1) * LANES] = _get_sequence(y_s, b, tm // nb, c)
    else:
        o_ref[...] = y
        ob_ref[...] = y.astype(BF16)


def _ffn(h, hb, l, p4, w_up, w_down, cw, cb, wpg, wple, g, b, alpha, tm, tf, batch_major_out):
    m, d = h.shape
    _, nb, seq, dple = p4.shape
    tt = tm // nb
    halo = BF16_ROWS
    per = tm // halo
    last = m // halo - 1
    row = pl.BlockSpec((tm, d), lambda i: (i, 0))
    if batch_major_out:
        out_specs = pl.BlockSpec((nb, tt, d), lambda i: (0, i, 0))
        out_shape = jax.ShapeDtypeStruct((nb, seq, d), F32)
        scratch = [_slab_scratch(tm, dple), _slab_scratch(tm, d)]
    else:
        out_specs = [row, row]
        out_shape = [jax.ShapeDtypeStruct((m, d), F32), jax.ShapeDtypeStruct((m, d), BF16)]
        scratch = [_slab_scratch(tm, dple)]
    return pl.pallas_call(
        functools.partial(_ffn_kernel, alpha=alpha, tf=tf, nb=nb, batch_major_out=batch_major_out),
        grid=(m // tm,),
        in_specs=[row,
                  pl.BlockSpec((halo, d), lambda i: (jnp.maximum(i * per - 1, 0), 0)),
                  pl.BlockSpec((halo, d), lambda i: (jnp.minimum((i + 1) * per, last), 0)),
                  row,
                  pl.BlockSpec((None, nb, tt, dple), lambda i: (l, 0, i, 0))]
                 + [_layer_spec(a, l) for a in (w_up, w_down, cw, cb, wpg, wple, g, b)],
        out_specs=out_specs,
        out_shape=out_shape,
        scratch_shapes=scratch,
        compiler_params=_cparams("parallel"),
        name="ffn_ple_ln",
    )(hb, hb, hb, h, p4, w_up, w_down, cw, cb, wpg, wple, g, b)


def _rope_tables(seq):
    half = ROPE_DIM // 2
    mid = HEAD_DIM // 2
    pos = jnp.arange(seq, dtype=F32)
    inv = ROPE_THETA ** (-jnp.arange(0, ROPE_DIM, 2, dtype=F32) / ROPE_DIM)
    ang = pos[:, None] * inv[None, :]
    cos, sin = jnp.cos(ang), jnp.sin(ang)
    ones = jnp.ones((seq, mid - half), F32)
    zeros = jnp.zeros((seq, mid - half), F32)
    c = jnp.concatenate([cos, ones, cos, ones], axis=1)
    s = jnp.concatenate([-sin, zeros, sin, zeros], axis=1)
    return c, s


def kernel(x, p, ln_in_g, ln_in_b, w_in, attn_sink, rnn_conv_w, rnn_conv_b, rg_w_a, rg_b_a, rg_w_x, rg_b_x, rg_lambda, w_proj_attn, w_proj_rnn, w_out, ln1_g, ln1_b, w_up, ffn_conv_w, ffn_conv_b, w_down, w_ple, w_ple_gate, ln2_g, ln2_b):
    nb, seq, d = x.shape
    depth = w_in.shape[0]
    m = nb * seq
    alpha = float((2 * depth) ** 0.25)
    n_qkv = (N_HEADS + 2 * N_KV_HEADS) * HEAD_DIM
    assert nb == SUBLANES and d == N_HEADS * HEAD_DIM and seq % WINDOW == 0 and seq >= 3 * WINDOW

    tm_ln = min(TM_LN, m)
    tm_proj = min(TM_PROJ, m)
    tm_merge = min(TM_MERGE, m)
    tm_ffn = min(TM_FFN, m)
    ts_rnn = min(TS_RNN, seq)

    w_a, w_z = _w_in_cast(w_in, n_qkv + d, min(TR_CAST, d))
    w_g = jnp.concatenate([rg_w_a, rg_w_x], axis=-1).astype(BF16)
    w_pa, w_pr, w_o = w_proj_attn.astype(BF16), w_proj_rnn.astype(BF16), w_out.astype(BF16)
    w_u, w_d = w_up.astype(BF16), w_down.astype(BF16)
    w_pg, w_pl = w_ple_gate.astype(BF16), w_ple.astype(BF16)

    row = lambda a: a.reshape(a.shape[:-1] + (1, a.shape[-1]))
    conv_w, conv_b = row(rnn_conv_w), row(rnn_conv_b)
    b_a, b_x, lam = row(rg_b_a), row(rg_b_x), row(rg_lambda)
    g1, b1, g2, b2, ffn_cb = row(ln1_g), row(ln1_b), row(ln2_g), row(ln2_b), row(ffn_conv_b)

    rope_c, rope_s = _rope_tables(seq)
    h, hb = _ln_in(x, ln_in_g, ln_in_b, tm_ln)

    for l in range(depth):
        qkv, xr = _in_proj(hb, l, w_a, rope_c, rope_s, nb, n_qkv, tm_proj, TN)
        hf, hbk = _rnn(xr, l, conv_w, conv_b, w_g, b_a, b_x, lam, nb, ts_rnn)
        attn = _attention(qkv, l, attn_sink)
        h, hb = _merge(hb, attn, hf, hbk, h, l, w_z, w_pa, w_pr, w_o, g1, b1, alpha, tm_merge)
        out = _ffn(h, hb, l, p, w_u, w_d, ffn_conv_w, ffn_cb, w_pg, w_pl, g2, b2,
                   alpha, tm_ffn, TF_FFN, batch_major_out=(l == depth - 1))
        if l < depth - 1:
            h, hb = out
    return out
```

```python
import functools
import math

import jax
import jax.numpy as jnp
from jax import lax
from jax.experimental import pallas as pl
from jax.experimental.pallas import tpu as pltpu

F32 = jnp.float32
BF16 = jnp.bfloat16

N_HEADS = 8
N_KV_HEADS = 2
HEAD_DIM = 128
Q_GROUP = N_HEADS // N_KV_HEADS
WINDOW = 128
ROPE_THETA = 500000.0
ROPE_DIM = HEAD_DIM // 4
N_RNN_BLOCKS = 4
RNN_CONV_W = 4
FFN_CONV_W = 3
RG_C = 8.0
LN_EPS = 1e-5
NEG_INF = -1e30
LOG2E = math.log2(math.e)
GELU_C = math.sqrt(2.0 / math.pi)
GELU_A = 0.044715

LANES = 128
SUBLANES = 8
BF16_ROWS = 16
MXU_COLS = 256
VMEM_LIMIT = 56 * 1024 * 1024

TT_LN = 512
TM_PROJ = 2048
TM_MERGE = 512
TM_FFN = 512
TS_RNN = 128
RNN_BLOCKS_PER_STEP = 2
TN = 512
TF_FFN = 1536
TR_CAST = 256


def _cparams(*sem):
    return pltpu.CompilerParams(dimension_semantics=sem, vmem_limit_bytes=VMEM_LIMIT)


def _layer_norm(x, g, b):
    mu = jnp.mean(x, axis=-1, keepdims=True)
    xc = x - mu
    var = jnp.mean(xc * xc, axis=-1, keepdims=True)
    return xc * lax.rsqrt(var + LN_EPS) * g + b


def _gelu(x):
    k1 = -2.0 * GELU_C * LOG2E
    return x / (1.0 + jnp.exp2(x * (k1 + (k1 * GELU_A) * (x * x))))


def _softplus(x):
    return jnp.maximum(x, 0.0) + jnp.log1p(jnp.exp(-jnp.abs(x)))


def _const_spec(shape):
    nd = len(shape)
    return pl.BlockSpec(shape, lambda *_: (0,) * nd, pipeline_mode=pl.Buffered(1))


def _layer_spec(arr, l):
    tail = (0,) * (arr.ndim - 1)
    return pl.BlockSpec((None,) + arr.shape[1:], lambda *_: (l,) + tail, pipeline_mode=pl.Buffered(1))


def _slab_scratch(rows, cols):
    return pltpu.VMEM((cols // LANES, rows, LANES), F32)


def _put_sequence(slab_s, b, val):
    tt, n = val.shape
    for c in range(n // LANES):
        slab_s[c, pl.ds(b, tt, stride=SUBLANES), :] = val[:, c * LANES:(c + 1) * LANES]


def _get_sequence(slab_s, b, tt, c):
    return slab_s[c, pl.ds(b, tt, stride=SUBLANES), :]


def _slabs(slab_s):
    return jnp.concatenate([slab_s[c] for c in range(slab_s.shape[0])], axis=1)


def _ln_in_kernel(x_ref, g_ref, b_ref, h_ref, hb_ref):
    y = _layer_norm(x_ref[...], g_ref[...], b_ref[...])
    h_ref[...] = y
    hb_ref[...] = y.astype(BF16)


def _ln_in(x, g, b, tt):
    nb, seq, d = x.shape
    m = nb * seq
    out = pl.BlockSpec((tt, d), lambda b_, i: (i, b_))
    h, hb = pl.pallas_call(
        _ln_in_kernel,
        grid=(nb, seq // tt),
        in_specs=[pl.BlockSpec((None, tt, d), lambda b_, i: (b_, i, 0)),
                  _const_spec((1, d)), _const_spec((1, d))],
        out_specs=[out, out],
        out_shape=[jax.ShapeDtypeStruct((seq, nb * d), F32), jax.ShapeDtypeStruct((seq, nb * d), BF16)],
        compiler_params=_cparams("parallel", "parallel"),
        name="ln_in",
    )(x, g.reshape(1, d), b.reshape(1, d))
    return h.reshape(m, d), hb.reshape(m, d)


def _w_in_cast_kernel(w_ref, wa_ref, wz_ref):
    half = ROPE_DIM // 2
    mid = HEAD_DIM // 2
    pieces = ((0, half), (ROPE_DIM, mid + half), (half, ROPE_DIM), (mid + half, HEAD_DIM))
    na = wa_ref.shape[1]
    for hd in range(N_HEADS + N_KV_HEADS):
        x = w_ref[:, hd * HEAD_DIM:(hd + 1) * HEAD_DIM]
        y = jnp.concatenate([x[:, a:b] for a, b in pieces], axis=1)
        wa_ref[:, hd * HEAD_DIM:(hd + 1) * HEAD_DIM] = y.astype(BF16)
    rest = (N_HEADS + N_KV_HEADS) * HEAD_DIM
    wa_ref[:, rest:] = w_ref[:, rest:na].astype(BF16)
    wz_ref[...] = w_ref[:, na:].astype(BF16)


def _w_in_cast(w_in, na, tr):
    depth, d, n = w_in.shape
    return pl.pallas_call(
        _w_in_cast_kernel,
        grid=(depth, d // tr),
        in_specs=[pl.BlockSpec((None, tr, n), lambda l, i: (l, i, 0))],
        out_specs=[pl.BlockSpec((None, tr, na), lambda l, i: (l, i, 0)),
                   pl.BlockSpec((None, tr, n - na), lambda l, i: (l, i, 0))],
        out_shape=[jax.ShapeDtypeStruct((depth, d, na), BF16),
                   jax.ShapeDtypeStruct((depth, d, n - na), BF16)],
        compiler_params=_cparams("parallel", "parallel"),
        name="w_in_cast",
    )(w_in)


def _in_proj_kernel(h_ref, w_ref, c_ref, s_ref, o_ref, xr_ref, *, n_qkv_tiles, n_rope_last):
    step = pl.program_id(1)
    j = step // 2
    is_xr = step % 2 == 1
    last = n_qkv_tiles - 1
    h = h_ref[...]

    @pl.when(is_xr)
    def _():
        xr_ref[...] = jnp.dot(h, w_ref[...], preferred_element_type=F32)

    @pl.when(jnp.logical_not(is_xr))
    def _():
        c, s = c_ref[...], s_ref[...]

        def emit(zc, cc, rotary):
            if rotary:
                zc = zc * c + pltpu.roll(zc, LANES // 2, 1) * s
            o_ref[:, cc * LANES:(cc + 1) * LANES] = zc.astype(BF16)

        heads_per_dot = MXU_COLS // LANES
        for cd in range(w_ref.shape[1] // MXU_COLS):
            z = jnp.dot(h, w_ref[:, cd * MXU_COLS:(cd + 1) * MXU_COLS], preferred_element_type=F32)
            for ch in range(heads_per_dot):
                cc = cd * heads_per_dot + ch
                zc = z[:, ch * LANES:(ch + 1) * LANES]
                if cc < n_rope_last:
                    emit(zc, cc, True)
                else:
                    pl.when(j < last)(functools.partial(emit, zc, cc, True))
                    pl.when(j == last)(functools.partial(emit, zc, cc, False))


def _in_proj(hb, l, w, rope_c, rope_s, n_qkv, tm, tn):
    m, d = hb.shape
    n = w.shape[2]
    n_rope_cols = (N_HEADS + N_KV_HEADS) * HEAD_DIM
    assert n % tn == 0 and n_qkv % tn == 0 and tn % MXU_COLS == 0 and (n_qkv - tn) <= n_rope_cols <= n_qkv
    nq = n_qkv // tn
    assert (n - n_qkv) // tn == nq - 1
    n_rope_last = (n_rope_cols - (n_qkv - tn)) // LANES
    tab = pl.BlockSpec((tm, LANES), lambda i, s: (i, 0))
    return pl.pallas_call(
        functools.partial(_in_proj_kernel, n_qkv_tiles=nq, n_rope_last=n_rope_last),
        grid=(m // tm, n // tn),
        in_specs=[pl.BlockSpec((tm, d), lambda i, s: (i, 0)),
                  pl.BlockSpec((None, d, tn), lambda i, s: (l, 0, s // 2 + (s % 2) * nq)),
                  tab, tab],
        out_specs=[pl.BlockSpec((tm, tn), lambda i, s: (i, s // 2)),
                   pl.BlockSpec((tm, tn), lambda i, s: (i, jnp.maximum(s - 1, 0) // 2))],
        out_shape=[jax.ShapeDtypeStruct((m, n_qkv), BF16),
                   jax.ShapeDtypeStruct((m, n - n_qkv), F32)],
        compiler_params=_cparams("parallel", "arbitrary"),
        name="in_proj",
    )(hb, w, rope_c, rope_s)


def _rnn_kernel(xf_ref, xb_ref, cw_ref, cb_ref, wg_ref, ba_ref, bx_ref, lam_ref,
                hf_ref, hb_ref, xsf_s, xsb_s, af_s, uf_s, ab_s, ub_s, carry_s, *, nb):
    i = pl.program_id(1)
    rows, bw = hf_ref.shape
    blk = wg_ref.shape[2]
    ts = rows // nb
    halo = (RNN_CONV_W - 1) * nb

    @pl.when(i == 0)
    def _():
        xsf_s[0:halo, :] = jnp.zeros((halo, bw), F32)
        xsb_s[rows:rows + halo, :] = jnp.zeros((halo, bw), F32)
        carry_s[...] = jnp.zeros_like(carry_s)

    def gates(d, half_x, start, a_s, u_s):
        xcb = half_x.astype(BF16)
        gs = [jnp.dot(xcb[:, sb * blk:(sb + 1) * blk], wg_ref[d, sb], preferred_element_type=F32)
              for sb in range(bw // blk)]
        g_a = jnp.concatenate([g[:, :blk] for g in gs], axis=1)
        g_x = jnp.concatenate([g[:, blk:] for g in gs], axis=1)
        t_a = jnp.tanh(g_a + ba_ref[d] * 0.5)
        t_x = jnp.tanh(g_x + bx_ref[d] * 0.5)
        half_c = (-0.5 * RG_C) * _softplus(-lam_ref[d])
        log_a = half_c + half_c * t_a
        a = jnp.exp(log_a)
        w = jnp.tanh(log_a) * (-1.0 - a * a)
        mult = jnp.where(w > 0.0, w * lax.rsqrt(w), 0.0)
        xg = half_x + half_x * t_x
        a_s[...] = a
        u_s[...] = xg * mult
        return xg[start, :]

    xsf_s[halo:halo + rows, :] = xf_ref[...]
    yf = cb_ref[0] * 0.5 + (cw_ref[0, 0] * 0.5) * xsf_s[halo:halo + rows, :]
    for k in range(1, RNN_CONV_W):
        yf = yf + (cw_ref[0, k] * 0.5) * xsf_s[halo - k * nb:halo - k * nb + rows, :]
    xsf_s[0:halo, :] = xsf_s[rows:rows + halo, :]
    first_f = gates(0, yf, slice(0, nb), af_s, uf_s)

    xsb_s[0:rows, :] = xb_ref[...]
    yb = cb_ref[1] * 0.5 + (cw_ref[1, 0] * 0.5) * xsb_s[0:rows, :]
    for k in range(1, RNN_CONV_W):
        yb = yb + (cw_ref[1, k] * 0.5) * xsb_s[k * nb:k * nb + rows, :]
    xsb_s[rows:rows + halo, :] = xsb_s[0:halo, :]
    first_b = gates(1, yb, slice(rows - nb, rows), ab_s, ub_s)

    @pl.when(i == 0)
    def _():
        uf_s[0:nb, :] = first_f
        ub_s[rows - nb:rows, :] = first_b

    def step(tt, carry):
        cf, cb = carry
        rf = pl.ds(pl.multiple_of(tt * nb, nb), nb)
        rb = pl.ds(pl.multiple_of((ts - 1 - tt) * nb, nb), nb)
        cf = af_s[rf, :] * cf + uf_s[rf, :]
        cb = ab_s[rb, :] * cb + ub_s[rb, :]
        hf_ref[rf, :] = cf
        hb_ref[rb, :] = cb
        return cf, cb

    cf, cb = lax.fori_loop(0, ts, step, (carry_s[0], carry_s[1]), unroll=8)
    carry_s[0] = cf
    carry_s[1] = cb


def _rnn(xr, l, conv_w, conv_b, wg, b_a, b_x, lam, nb, ts):
    m, d = xr.shape
    blk = d // N_RNN_BLOCKS
    bw = RNN_BLOCKS_PER_STEP * blk
    rows = ts * nb
    nt = m // rows
    halo = (RNN_CONV_W - 1) * nb
    xs = pltpu.VMEM((rows + halo, bw), F32)
    au = pltpu.VMEM((rows, bw), F32)
    vec = lambda: pl.BlockSpec((None, 2, 1, bw), lambda c, i: (l, 0, 0, c))
    return pl.pallas_call(
        functools.partial(_rnn_kernel, nb=nb),
        grid=(d // bw, nt),
        in_specs=[pl.BlockSpec((rows, bw), lambda c, i: (i, c)),
                  pl.BlockSpec((rows, bw), lambda c, i: (nt - 1 - i, c)),
                  pl.BlockSpec((None, 2, RNN_CONV_W, 1, bw), lambda c, i: (l, 0, 0, 0, c)),
                  vec(),
                  pl.BlockSpec((None, 2, RNN_BLOCKS_PER_STEP, blk, 2 * blk), lambda c, i: (l, 0, c, 0, 0)),
                  vec(), vec(), vec()],
        out_specs=[pl.BlockSpec((rows, bw), lambda c, i: (i, c)),
                   pl.BlockSpec((rows, bw), lambda c, i: (nt - 1 - i, c))],
        out_shape=[jax.ShapeDtypeStruct((m, d), F32)] * 2,
        scratch_shapes=[xs, xs, au, au, au, au, pltpu.VMEM((2, nb, bw), F32)],
        compiler_params=_cparams("parallel", "arbitrary"),
        name="rg_lru",
    )(xr, xr, conv_w, conv_b, wg, b_a, b_x, lam)


def _attn_kernel(sink_ref, q_ref, k_ref, v_ref, o_ref, kp_s, vp_s, s_s, p_s, *, l):
    hk = pl.program_id(1)
    seq = q_ref.shape[0]
    blk = WINDOW
    nblk = seq // blk
    span = 3 * blk
    c1 = HEAD_DIM ** -0.5 * LOG2E

    kp_s[0] = jnp.zeros((HEAD_DIM, blk), BF16)
    for jb in range(nblk):
        kp_s[jb + 1] = k_ref[jb * blk:(jb + 1) * blk, :].astype(F32).T.astype(BF16)
    kp_s[nblk + 1] = jnp.zeros((HEAD_DIM, blk), BF16)
    vp_s[0:blk, 0:HEAD_DIM] = jnp.zeros((blk, HEAD_DIM), BF16)
    vp_s[blk:blk + seq, 0:HEAD_DIM] = v_ref[...]
    vp_s[blk + seq:2 * blk + seq, 0:HEAD_DIM] = jnp.zeros((blk, HEAD_DIM), BF16)
    vp_s[:, HEAD_DIM:] = jnp.ones((seq + 2 * blk, HEAD_DIM), BF16)

    qi = lax.broadcasted_iota(jnp.int32, (blk, blk), 0)
    kj = lax.broadcasted_iota(jnp.int32, (blk, blk), 1)
    bias_prev = jnp.where(kj >= qi, 0.0, NEG_INF)
    bias_next = jnp.where(kj <= qi, 0.0, NEG_INF)

    def logits(n, slot):
        q0 = pl.multiple_of(n * blk, blk)
        qs = jnp.concatenate(
            [q_ref[pl.ds(q0, blk), g * HEAD_DIM:(g + 1) * HEAD_DIM] for g in range(Q_GROUP)], axis=0)
        kw = jnp.concatenate([kp_s[n], kp_s[n + 1], kp_s[n + 2]], axis=1)
        s_s[slot] = jnp.dot(qs, kw, preferred_element_type=F32)

    def softmax(n, slot):
        bp = jnp.where(n == 0, NEG_INF, bias_prev)
        bn = jnp.where(n == nblk - 1, NEG_INF, bias_next)
        m2s = []
        for g in range(Q_GROUP):
            r = slice(g * blk, (g + 1) * blk)
            s_prev = s_s[slot, r, 0:blk] + bp
            s_cur = s_s[slot, r, blk:2 * blk]
            s_next = s_s[slot, r, 2 * blk:span] + bn
            m_raw = jnp.max(jnp.maximum(jnp.maximum(s_prev, s_cur), s_next), axis=-1, keepdims=True)
            m2 = jnp.maximum(m_raw * c1, sink_ref[l, hk * Q_GROUP + g] * LOG2E)
            p_s[slot, r, 0:blk] = jnp.exp2(s_prev * c1 - m2).astype(BF16)
            p_s[slot, r, blk:2 * blk] = jnp.exp2(s_cur * c1 - m2).astype(BF16)
            p_s[slot, r, 2 * blk:span] = jnp.exp2(s_next * c1 - m2).astype(BF16)
            m2s.append(m2)
        return tuple(m2s)

    def values(n, slot, m2s):
        q0 = n * blk if isinstance(n, int) else pl.multiple_of(n * blk, blk)
        oe = jnp.dot(p_s[slot], vp_s[pl.ds(q0, span), :], preferred_element_type=F32)
        for g in range(Q_GROUP):
            r = slice(g * blk, (g + 1) * blk)
            den = oe[r, HEAD_DIM:] + jnp.exp2(sink_ref[l, hk * Q_GROUP + g] * LOG2E - m2s[g])
            o_ref[pl.ds(q0, blk), g * HEAD_DIM:(g + 1) * HEAD_DIM] = (oe[r, :HEAD_DIM] / den).astype(BF16)

    logits(0, 0)
    p_s[1] = jnp.zeros(p_s.shape[1:], BF16)

    def body(i, m_prev):
        n = 2 * i
        logits(n + 1, 1)
        m_even = softmax(n, 0)
        values(jnp.maximum(n - 1, 0), 1, m_prev)
        logits(jnp.minimum(n + 2, nblk - 1), 0)
        m_odd = softmax(n + 1, 1)
        values(n, 0, m_even)
        return m_odd

    assert nblk % 2 == 0
    m_init = tuple(jnp.zeros((blk, 1), F32) for _ in range(Q_GROUP))
    m_last = lax.fori_loop(0, nblk // 2, body, m_init)
    values(nblk - 1, 1, m_last)


def _attention(qkv, l, sink, nb):
    m, n_qkv = qkv.shape
    seq = m // nb
    d = N_HEADS * HEAD_DIM
    gw = Q_GROUP * HEAD_DIM
    q_per_seq = n_qkv // gw
    h_per_seq = n_qkv // HEAD_DIM
    k_blk0 = N_HEADS
    v_blk0 = N_HEADS + N_KV_HEADS
    rows = Q_GROUP * WINDOW
    qkv2 = qkv.reshape(seq, nb * n_qkv)
    out = pl.pallas_call(
        functools.partial(_attn_kernel, l=l),
        grid=(nb, N_KV_HEADS),
        in_specs=[pl.BlockSpec(memory_space=pltpu.SMEM),
                  pl.BlockSpec((seq, gw), lambda b, h: (0, b * q_per_seq + h)),
                  pl.BlockSpec((seq, HEAD_DIM), lambda b, h: (0, b * h_per_seq + k_blk0 + h)),
                  pl.BlockSpec((seq, HEAD_DIM), lambda b, h: (0, b * h_per_seq + v_blk0 + h))],
        out_specs=pl.BlockSpec((seq, gw), lambda b, h: (0, b * (d // gw) + h)),
        out_shape=jax.ShapeDtypeStruct((seq, nb * d), BF16),
        scratch_shapes=[pltpu.VMEM((seq // WINDOW + 2, HEAD_DIM, WINDOW), BF16),
                        pltpu.VMEM((seq + 2 * WINDOW, 2 * HEAD_DIM), BF16),
                        pltpu.VMEM((2, rows, 3 * WINDOW), F32),
                        pltpu.VMEM((2, rows, 3 * WINDOW), BF16)],
        compiler_params=_cparams("parallel", "parallel"),
        name="swa_attention",
    )(sink, qkv2, qkv2, qkv2)
    return out.reshape(m, d)


def _merge_kernel(hb_ref, attn_ref, hf_ref, hbk_ref, h_ref,
                  wz_ref, wpa_ref, wpr_ref, wo_ref, g_ref, b_ref, o_ref, ob_ref, *, alpha):
    d = h_ref.shape[1]
    z = jnp.dot(hb_ref[...], wz_ref[...], preferred_element_type=F32)
    rnn = (hf_ref[...] + hbk_ref[...]) * _gelu(z[:, :d])
    pa = jnp.dot(attn_ref[...], wpa_ref[...], preferred_element_type=F32)
    pr = jnp.dot(rnn.astype(BF16), wpr_ref[...], preferred_element_type=F32)
    merged = jax.nn.sigmoid(z[:, d:2 * d]) * pa + jax.nn.sigmoid(z[:, 2 * d:]) * pr
    o = jnp.dot(merged.astype(BF16), wo_ref[...], preferred_element_type=F32)
    y = _layer_norm(alpha * h_ref[...] + o, g_ref[...], b_ref[...])
    o_ref[...] = y
    ob_ref[...] = y.astype(BF16)


def _merge(hb, attn, hf, hbk, h, l, wz, wpa, wpr, wo, g, b, alpha, tm):
    m, d = h.shape
    row = pl.BlockSpec((tm, d), lambda i: (i, 0))
    return pl.pallas_call(
        functools.partial(_merge_kernel, alpha=alpha),
        grid=(m // tm,),
        in_specs=[row, row, row, row, row] + [_layer_spec(a, l) for a in (wz, wpa, wpr, wo, g, b)],
        out_specs=[row, row],
        out_shape=[jax.ShapeDtypeStruct((m, d), F32), jax.ShapeDtypeStruct((m, d), BF16)],
        compiler_params=_cparams("parallel"),
        name="merge_out_ln",
    )(hb, attn, hf, hbk, h, wz, wpa, wpr, wo, g, b)


def _ffn_kernel(hb_ref, hprev_ref, hnext_ref, h_ref, p_ref, wu_ref, wd_ref, cw_ref, cb_ref,
                wpg_ref, wple_ref, g_ref, b_ref, *rest, alpha, tf, nb, batch_major_out):
    if batch_major_out:
        o_ref, p_s, y_s = rest
    else:
        o_ref, ob_ref, p_s = rest
    i = pl.program_id(0)
    tm = hb_ref.shape[0]
    dff = wd_ref.shape[0]
    halo = hprev_ref.shape[0]
    hb = hb_ref[...]
    hp = hprev_ref[...]
    hn = hnext_ref[...]
    hp = jnp.where(i == 0, jnp.zeros_like(hp), hp)
    hn = jnp.where(i == pl.num_programs(0) - 1, jnp.zeros_like(hn), hn)
    lhs = jnp.concatenate([hp, hb, hn], axis=0)
    acc = None
    for j in range(dff // tf):
        cs = slice(j * tf, (j + 1) * tf)
        gate = jnp.dot(lhs, wu_ref[:, cs], preferred_element_type=F32)
        val = jnp.dot(hb, wu_ref[:, dff + j * tf:dff + (j + 1) * tf], preferred_element_type=F32)
        conv = (cw_ref[0:1, cs] * gate[halo - nb:halo - nb + tm] + cw_ref[1:2, cs] * gate[halo:halo + tm]
                + cw_ref[2:3, cs] * gate[halo + nb:halo + nb + tm] + cb_ref[:, cs])
        act = (_gelu(conv) * val).astype(BF16)
        pc = jnp.dot(act, wd_ref[cs, :], preferred_element_type=F32)
        acc = pc if acc is None else acc + pc

    for b in range(nb):
        _put_sequence(p_s, b, p_ref[b])
    ple = (jax.nn.sigmoid(jnp.dot(hb, wpg_ref[...], preferred_element_type=F32))
           * jnp.dot(_slabs(p_s).astype(BF16), wple_ref[...], preferred_element_type=F32))
    y = _layer_norm(alpha * h_ref[...] + acc + ple, g_ref[...], b_ref[...])
    if batch_major_out:
        for c in range(y_s.shape[0]):
            y_s[c] = y[:, c * LANES:(c + 1) * LANES]
        for b in range(nb):
            for c in range(y_s.shape[0]):
                o_ref[b, :, c * LANES:(c + 1) * LANES] = _get_sequence(y_s, b, tm // nb, c)
    else:
        o_ref[...] = y
        ob_ref[...] = y.astype(BF16)


def _ffn(h, hb, l, p4, w_up, w_down, cw, cb, wpg, wple, g, b, alpha, tm, tf, batch_major_out):
    m, d = h.shape
    _, nb, seq, dple = p4.shape
    tt = tm // nb
    halo = BF16_ROWS
    per = tm // halo
    last = m // halo - 1
    row = pl.BlockSpec((tm, d), lambda i: (i, 0))
    if batch_major_out:
        out_specs = pl.BlockSpec((nb, tt, d), lambda i: (0, i, 0))
        out_shape = jax.ShapeDtypeStruct((nb, seq, d), F32)
        scratch = [_slab_scratch(tm, dple), _slab_scratch(tm, d)]
    else:
        out_specs = [row, row]
        out_shape = [jax.ShapeDtypeStruct((m, d), F32), jax.ShapeDtypeStruct((m, d), BF16)]
        scratch = [_slab_scratch(tm, dple)]
    return pl.pallas_call(
        functools.partial(_ffn_kernel, alpha=alpha, tf=tf, nb=nb, batch_major_out=batch_major_out),
        grid=(m // tm,),
        in_specs=[row,
                  pl.BlockSpec((halo, d), lambda i: (jnp.maximum(i * per - 1, 0), 0)),
                  pl.BlockSpec((halo, d), lambda i: (jnp.minimum((i + 1) * per, last), 0)),
                  row,
                  pl.BlockSpec((None, nb, tt, dple), lambda i: (l, 0, i, 0))]
                 + [_layer_spec(a, l) for a in (w_up, w_down, cw, cb, wpg, wple, g, b)],
        out_specs=out_specs,
        out_shape=out_shape,
        scratch_shapes=scratch,
        compiler_params=_cparams("parallel"),
        name="ffn_ple_ln",
    )(hb, hb, hb, h, p4, w_up, w_down, cw, cb, wpg, wple, g, b)


def _rope_tables(seq):
    half = ROPE_DIM // 2
    mid = HEAD_DIM // 2
    pos = jnp.arange(seq, dtype=F32)
    inv = ROPE_THETA ** (-jnp.arange(0, ROPE_DIM, 2, dtype=F32) / ROPE_DIM)
    ang = pos[:, None] * inv[None, :]
    cos, sin = jnp.cos(ang), jnp.sin(ang)
    ones = jnp.ones((seq, mid - half), F32)
    zeros = jnp.zeros((seq, mid - half), F32)
    c = jnp.concatenate([cos, ones, cos, ones], axis=1)
    s = jnp.concatenate([-sin, zeros, sin, zeros], axis=1)
    return c, s


def kernel(x, p, ln_in_g, ln_in_b, w_in, attn_sink, rnn_conv_w, rnn_conv_b, rg_w_a, rg_b_a, rg_w_x, rg_b_x, rg_lambda, w_proj_attn, w_proj_rnn, w_out, ln1_g, ln1_b, w_up, ffn_conv_w, ffn_conv_b, w_down, w_ple, w_ple_gate, ln2_g, ln2_b):
    nb, seq, d = x.shape
    depth = w_in.shape[0]
    m = nb * seq
    alpha = float((2 * depth) ** 0.25)
    n_qkv = (N_HEADS + 2 * N_KV_HEADS) * HEAD_DIM
    assert nb == SUBLANES and d == N_HEADS * HEAD_DIM and seq % WINDOW == 0 and seq >= 3 * WINDOW

    tt_ln = min(TT_LN, seq)
    tm_proj = min(TM_PROJ, m)
    tm_merge = min(TM_MERGE, m)
    tm_ffn = min(TM_FFN, m)
    ts_rnn = min(TS_RNN, seq)

    w_a, w_z = _w_in_cast(w_in, n_qkv + d, min(TR_CAST, d))
    w_g = jnp.concatenate([rg_w_a, rg_w_x], axis=-1).astype(BF16)
    w_pa, w_pr, w_o = w_proj_attn.astype(BF16), w_proj_rnn.astype(BF16), w_out.astype(BF16)
    w_u, w_d = w_up.astype(BF16), w_down.astype(BF16)
    w_pg, w_pl = w_ple_gate.astype(BF16), w_ple.astype(BF16)

    row = lambda a: a.reshape(a.shape[:-1] + (1, a.shape[-1]))
    conv_w, conv_b = row(rnn_conv_w), row(rnn_conv_b)
    b_a, b_x, lam = row(rg_b_a), row(rg_b_x), row(rg_lambda)
    g1, b1, g2, b2, ffn_cb = row(ln1_g), row(ln1_b), row(ln2_g), row(ln2_b), row(ffn_conv_b)

    rope_c, rope_s = (jnp.repeat(t, nb, axis=0) for t in _rope_tables(seq))
    h, hb = _ln_in(x, ln_in_g, ln_in_b, tt_ln)

    for l in range(depth):
        qkv, xr = _in_proj(hb, l, w_a, rope_c, rope_s, n_qkv, tm_proj, TN)
        hf, hbk = _rnn(xr, l, conv_w, conv_b, w_g, b_a, b_x, lam, nb, ts_rnn)
        attn = _attention(qkv, l, attn_sink, nb)
        h, hb = _merge(hb, attn, hf, hbk, h, l, w_z, w_pa, w_pr, w_o, g1, b1, alpha, tm_merge)
        out = _ffn(h, hb, l, p, w_u, w_d, ffn_conv_w, ffn_cb, w_pg, w_pl, g2, b2,
                   alpha, tm_ffn, TF_FFN, batch_major_out=(l == depth - 1))
        if l < depth - 1:
            h, hb = out
    return out
```

```python
import functools
import math

import jax
import jax.numpy as jnp
from jax import lax
from jax.experimental import pallas as pl
from jax.experimental.pallas import tpu as pltpu

F32 = jnp.float32
BF16 = jnp.bfloat16

N_HEADS = 8
N_KV_HEADS = 2
HEAD_DIM = 128
Q_GROUP = N_HEADS // N_KV_HEADS
WINDOW = 128
ROPE_THETA = 500000.0
ROPE_DIM = HEAD_DIM // 4
N_RNN_BLOCKS = 4
RNN_CONV_W = 4
FFN_CONV_W = 3
RG_C = 8.0
LN_EPS = 1e-5
NEG_INF = -1e30
LOG2E = math.log2(math.e)
GELU_C = math.sqrt(2.0 / math.pi)
GELU_A = 0.044715

LANES = 128
SUBLANES = 8
BF16_ROWS = 16
MXU_COLS = 256
VMEM_LIMIT = 56 * 1024 * 1024

TM_LN = 1024
TM_PROJ = 2048
TM_MERGE = 512
TM_FFN = 512
TS_RNN = 64
RNN_BLOCKS_PER_STEP = 4
TN = 512
TF_FFN = 1536
TR_CAST = 256


def _cparams(*sem):
    return pltpu.CompilerParams(dimension_semantics=sem, vmem_limit_bytes=VMEM_LIMIT)


def _layer_norm(x, g, b):
    mu = jnp.mean(x, axis=-1, keepdims=True)
    xc = x - mu
    var = jnp.mean(xc * xc, axis=-1, keepdims=True)
    return xc * lax.rsqrt(var + LN_EPS) * g + b


def _gelu(x):
    k1 = -2.0 * GELU_C * LOG2E
    return x / (1.0 + jnp.exp2(x * (k1 + (k1 * GELU_A) * (x * x))))


def _softplus(x):
    return jnp.maximum(x, 0.0) + jnp.log1p(jnp.exp(-jnp.abs(x)))


def _const_spec(shape):
    nd = len(shape)
    return pl.BlockSpec(shape, lambda *_: (0,) * nd, pipeline_mode=pl.Buffered(1))


def _layer_spec(arr, l):
    tail = (0,) * (arr.ndim - 1)
    return pl.BlockSpec((None,) + arr.shape[1:], lambda *_: (l,) + tail, pipeline_mode=pl.Buffered(1))


def _slab_scratch(rows, cols):
    return pltpu.VMEM((cols // LANES, rows, LANES), F32)


def _put_sequence(slab_s, b, val):
    tt, n = val.shape
    for c in range(n // LANES):
        slab_s[c, pl.ds(b, tt, stride=SUBLANES), :] = val[:, c * LANES:(c + 1) * LANES]


def _get_sequence(slab_s, b, tt, c):
    return slab_s[c, pl.ds(b, tt, stride=SUBLANES), :]


def _slabs(slab_s):
    return jnp.concatenate([slab_s[c] for c in range(slab_s.shape[0])], axis=1)


def _ln_in_kernel(x_ref, g_ref, b_ref, h_ref, hb_ref, t_s):
    nb = x_ref.shape[0]
    for b in range(nb):
        _put_sequence(t_s, b, _layer_norm(x_ref[b], g_ref[...], b_ref[...]))
    y = _slabs(t_s)
    h_ref[...] = y
    hb_ref[...] = y.astype(BF16)


def _ln_in(x, g, b, tm):
    nb, seq, d = x.shape
    m = nb * seq
    tt = tm // nb
    return pl.pallas_call(
        _ln_in_kernel,
        grid=(m // tm,),
        in_specs=[pl.BlockSpec((nb, tt, d), lambda i: (0, i, 0)),
                  _const_spec((1, d)), _const_spec((1, d))],
        out_specs=[pl.BlockSpec((tm, d), lambda i: (i, 0)),
                   pl.BlockSpec((tm, d), lambda i: (i, 0))],
        out_shape=[jax.ShapeDtypeStruct((m, d), F32), jax.ShapeDtypeStruct((m, d), BF16)],
        scratch_shapes=[_slab_scratch(tm, d)],
        compiler_params=_cparams("parallel"),
        name="ln_in",
    )(x, g.reshape(1, d), b.reshape(1, d))


def _w_in_cast_kernel(w_ref, wa_ref, wz_ref):
    half = ROPE_DIM // 2
    mid = HEAD_DIM // 2
    pieces = ((0, half), (ROPE_DIM, mid + half), (half, ROPE_DIM), (mid + half, HEAD_DIM))
    na = wa_ref.shape[1]
    for hd in range(N_HEADS + N_KV_HEADS):
        x = w_ref[:, hd * HEAD_DIM:(hd + 1) * HEAD_DIM]
        y = jnp.concatenate([x[:, a:b] for a, b in pieces], axis=1)
        wa_ref[:, hd * HEAD_DIM:(hd + 1) * HEAD_DIM] = y.astype(BF16)
    rest = (N_HEADS + N_KV_HEADS) * HEAD_DIM
    wa_ref[:, rest:] = w_ref[:, rest:na].astype(BF16)
    wz_ref[...] = w_ref[:, na:].astype(BF16)


def _w_in_cast(w_in, na, tr):
    depth, d, n = w_in.shape
    return pl.pallas_call(
        _w_in_cast_kernel,
        grid=(depth, d // tr),
        in_specs=[pl.BlockSpec((None, tr, n), lambda l, i: (l, i, 0))],
        out_specs=[pl.BlockSpec((None, tr, na), lambda l, i: (l, i, 0)),
                   pl.BlockSpec((None, tr, n - na), lambda l, i: (l, i, 0))],
        out_shape=[jax.ShapeDtypeStruct((depth, d, na), BF16),
                   jax.ShapeDtypeStruct((depth, d, n - na), BF16)],
        compiler_params=_cparams("parallel", "parallel"),
        name="w_in_cast",
    )(w_in)


def _in_proj_kernel(h_ref, w_ref, c_ref, s_ref, o_ref, xr_ref, *z_scratch, n_qkv_tiles, n_rope_last):
    step = pl.program_id(1)
    j = step // 2
    is_xr = step % 2 == 1
    last = n_qkv_tiles - 1
    nb, tt, _ = o_ref.shape
    h = h_ref[...]

    @pl.when(is_xr)
    def _():
        xr_ref[...] = jnp.dot(h, w_ref[...], preferred_element_type=F32)

    @pl.when(jnp.logical_not(is_xr))
    def _():
        c, s = c_ref[...], s_ref[...]

        def rope(zc):
            return zc * c + pltpu.roll(zc, LANES // 2, 1) * s

        heads_per_dot = MXU_COLS // LANES

        def emit(z_s, ch, cc, rotary):
            for b in range(nb):
                zc = _get_sequence(z_s, b, tt, ch)
                o_ref[b, :, cc * LANES:(cc + 1) * LANES] = (rope(zc) if rotary else zc).astype(BF16)

        for cd, z_s in enumerate(z_scratch):
            z = jnp.dot(h, w_ref[:, cd * MXU_COLS:(cd + 1) * MXU_COLS], preferred_element_type=F32)
            for ch in range(heads_per_dot):
                z_s[ch] = z[:, ch * LANES:(ch + 1) * LANES]
            for ch in range(heads_per_dot):
                cc = cd * heads_per_dot + ch
                if cc < n_rope_last:
                    emit(z_s, ch, cc, True)
                else:
                    pl.when(j < last)(functools.partial(emit, z_s, ch, cc, True))
                    pl.when(j == last)(functools.partial(emit, z_s, ch, cc, False))


def _in_proj(hb, l, w, rope_c, rope_s, nb, n_qkv, tm, tn):
    m, d = hb.shape
    n = w.shape[2]
    seq = m // nb
    tt = tm // nb
    n_rope_cols = (N_HEADS + N_KV_HEADS) * HEAD_DIM
    assert n % tn == 0 and n_qkv % tn == 0 and tn % MXU_COLS == 0 and (n_qkv - tn) <= n_rope_cols <= n_qkv
    nq = n_qkv // tn
    assert (n - n_qkv) // tn == nq - 1
    n_rope_last = (n_rope_cols - (n_qkv - tn)) // LANES
    tab = pl.BlockSpec((tt, LANES), lambda i, j: (i, 0))
    return pl.pallas_call(
        functools.partial(_in_proj_kernel, n_qkv_tiles=nq, n_rope_last=n_rope_last),
        grid=(m // tm, n // tn),
        in_specs=[pl.BlockSpec((tm, d), lambda i, s: (i, 0)),
                  pl.BlockSpec((None, d, tn), lambda i, s: (l, 0, s // 2 + (s % 2) * nq)),
                  tab, tab],
        out_specs=[pl.BlockSpec((nb, tt, tn), lambda i, s: (0, i, s // 2)),
                   pl.BlockSpec((tm, tn), lambda i, s: (i, jnp.maximum(s - 1, 0) // 2))],
        out_shape=[jax.ShapeDtypeStruct((nb, seq, n_qkv), BF16),
                   jax.ShapeDtypeStruct((m, n - n_qkv), F32)],
        scratch_shapes=[_slab_scratch(tm, MXU_COLS)] * (tn // MXU_COLS),
        compiler_params=_cparams("parallel", "arbitrary"),
        name="in_proj",
    )(hb, w, rope_c, rope_s)


def _rnn_kernel(xf_ref, xb_ref, cw_ref, cb_ref, wg_ref, ba_ref, bx_ref, lam_ref,
                hf_ref, hb_ref, xsf_s, xsb_s, af_s, uf_s, ab_s, ub_s, carry_s, *, nb):
    i = pl.program_id(1)
    rows, bw = hf_ref.shape
    blk = wg_ref.shape[2]
    ts = rows // nb
    halo = (RNN_CONV_W - 1) * nb

    @pl.when(i == 0)
    def _():
        xsf_s[0:halo, :] = jnp.zeros((halo, bw), F32)
        xsb_s[rows:rows + halo, :] = jnp.zeros((halo, bw), F32)
        carry_s[...] = jnp.zeros_like(carry_s)

    def gates(d, half_x, start, a_s, u_s):
        xcb = half_x.astype(BF16)
        gs = [jnp.dot(xcb[:, sb * blk:(sb + 1) * blk], wg_ref[d, sb], preferred_element_type=F32)
              for sb in range(bw // blk)]
        g_a = jnp.concatenate([g[:, :blk] for g in gs], axis=1)
        g_x = jnp.concatenate([g[:, blk:] for g in gs], axis=1)
        t_a = jnp.tanh(g_a + ba_ref[d] * 0.5)
        t_x = jnp.tanh(g_x + bx_ref[d] * 0.5)
        half_c = (-0.5 * RG_C) * _softplus(-lam_ref[d])
        log_a = half_c + half_c * t_a
        a = jnp.exp(log_a)
        w = jnp.tanh(log_a) * (-1.0 - a * a)
        mult = jnp.where(w > 0.0, w * lax.rsqrt(w), 0.0)
        xg = half_x + half_x * t_x
        a_s[...] = a
        u_s[...] = xg * mult
        return xg[start, :]

    xsf_s[halo:halo + rows, :] = xf_ref[...]
    yf = cb_ref[0] * 0.5 + (cw_ref[0, 0] * 0.5) * xsf_s[halo:halo + rows, :]
    for k in range(1, RNN_CONV_W):
        yf = yf + (cw_ref[0, k] * 0.5) * xsf_s[halo - k * nb:halo - k * nb + rows, :]
    xsf_s[0:halo, :] = xsf_s[rows:rows + halo, :]
    first_f = gates(0, yf, slice(0, nb), af_s, uf_s)

    xsb_s[0:rows, :] = xb_ref[...]
    yb = cb_ref[1] * 0.5 + (cw_ref[1, 0] * 0.5) * xsb_s[0:rows, :]
    for k in range(1, RNN_CONV_W):
        yb = yb + (cw_ref[1, k] * 0.5) * xsb_s[k * nb:k * nb + rows, :]
    xsb_s[rows:rows + halo, :] = xsb_s[0:halo, :]
    first_b = gates(1, yb, slice(rows - nb, rows), ab_s, ub_s)

    @pl.when(i == 0)
    def _():
        uf_s[0:nb, :] = first_f
        ub_s[rows - nb:rows, :] = first_b

    def step(tt, carry):
        cf, cb = carry
        rf = pl.ds(pl.multiple_of(tt * nb, nb), nb)
        rb = pl.ds(pl.multiple_of((ts - 1 - tt) * nb, nb), nb)
        cf = af_s[rf, :] * cf + uf_s[rf, :]
        cb = ab_s[rb, :] * cb + ub_s[rb, :]
        hf_ref[rf, :] = cf
        hb_ref[rb, :] = cb
        return cf, cb

    cf, cb = lax.fori_loop(0, ts, step, (carry_s[0], carry_s[1]), unroll=8)
    carry_s[0] = cf
    carry_s[1] = cb


def _rnn(xr, l, conv_w, conv_b, wg, b_a, b_x, lam, nb, ts):
    m, d = xr.shape
    blk = d // N_RNN_BLOCKS
    bw = RNN_BLOCKS_PER_STEP * blk
    rows = ts * nb
    nt = m // rows
    halo = (RNN_CONV_W - 1) * nb
    xs = pltpu.VMEM((rows + halo, bw), F32)
    au = pltpu.VMEM((rows, bw), F32)
    vec = lambda: pl.BlockSpec((None, 2, 1, bw), lambda c, i: (l, 0, 0, c))
    return pl.pallas_call(
        functools.partial(_rnn_kernel, nb=nb),
        grid=(d // bw, nt),
        in_specs=[pl.BlockSpec((rows, bw), lambda c, i: (i, c)),
                  pl.BlockSpec((rows, bw), lambda c, i: (nt - 1 - i, c)),
                  pl.BlockSpec((None, 2, RNN_CONV_W, 1, bw), lambda c, i: (l, 0, 0, 0, c)),
                  vec(),
                  pl.BlockSpec((None, 2, RNN_BLOCKS_PER_STEP, blk, 2 * blk), lambda c, i: (l, 0, c, 0, 0)),
                  vec(), vec(), vec()],
        out_specs=[pl.BlockSpec((rows, bw), lambda c, i: (i, c)),
                   pl.BlockSpec((rows, bw), lambda c, i: (nt - 1 - i, c))],
        out_shape=[jax.ShapeDtypeStruct((m, d), F32)] * 2,
        scratch_shapes=[xs, xs, au, au, au, au, pltpu.VMEM((2, nb, bw), F32)],
        compiler_params=_cparams("parallel", "arbitrary"),
        name="rg_lru",
    )(xr, xr, conv_w, conv_b, wg, b_a, b_x, lam)


def _attn_kernel(sink_ref, q_ref, k_ref, v_ref, o_ref, kp_s, vp_s, s_s, p_s, *, l):
    hk = pl.program_id(1)
    seq = q_ref.shape[1]
    blk = WINDOW
    nblk = seq // blk
    span = 3 * blk
    c1 = HEAD_DIM ** -0.5 * LOG2E

    kp_s[0] = jnp.zeros((HEAD_DIM, blk), BF16)
    for jb in range(nblk):
        kp_s[jb + 1] = k_ref[0, jb * blk:(jb + 1) * blk, :].T
    kp_s[nblk + 1] = jnp.zeros((HEAD_DIM, blk), BF16)
    vp_s[0:blk, 0:HEAD_DIM] = jnp.zeros((blk, HEAD_DIM), BF16)
    vp_s[blk:blk + seq, 0:HEAD_DIM] = v_ref[0]
    vp_s[blk + seq:2 * blk + seq, 0:HEAD_DIM] = jnp.zeros((blk, HEAD_DIM), BF16)
    vp_s[:, HEAD_DIM:] = jnp.ones((seq + 2 * blk, HEAD_DIM), BF16)

    qi = lax.broadcasted_iota(jnp.int32, (blk, blk), 0)
    kj = lax.broadcasted_iota(jnp.int32, (blk, blk), 1)
    bias_prev = jnp.where(kj >= qi, 0.0, NEG_INF)
    bias_next = jnp.where(kj <= qi, 0.0, NEG_INF)

    def logits(n, slot):
        q0 = pl.multiple_of(n * blk, blk)
        qs = jnp.concatenate(
            [q_ref[0, pl.ds(q0, blk), g * HEAD_DIM:(g + 1) * HEAD_DIM] for g in range(Q_GROUP)], axis=0)
        kw = jnp.concatenate([kp_s[n], kp_s[n + 1], kp_s[n + 2]], axis=1)
        s_s[slot] = jnp.dot(qs, kw, preferred_element_type=F32)

    def softmax(n, slot):
        bp = jnp.where(n == 0, NEG_INF, bias_prev)
        bn = jnp.where(n == nblk - 1, NEG_INF, bias_next)
        m2s = []
        for g in range(Q_GROUP):
            r = slice(g * blk, (g + 1) * blk)
            s_prev = s_s[slot, r, 0:blk] + bp
            s_cur = s_s[slot, r, blk:2 * blk]
            s_next = s_s[slot, r, 2 * blk:span] + bn
            m_raw = jnp.max(jnp.maximum(jnp.maximum(s_prev, s_cur), s_next), axis=-1, keepdims=True)
            m2 = jnp.maximum(m_raw * c1, sink_ref[l, hk * Q_GROUP + g] * LOG2E)
            p_s[slot, r, 0:blk] = jnp.exp2(s_prev * c1 - m2).astype(BF16)
            p_s[slot, r, blk:2 * blk] = jnp.exp2(s_cur * c1 - m2).astype(BF16)
            p_s[slot, r, 2 * blk:span] = jnp.exp2(s_next * c1 - m2).astype(BF16)
            m2s.append(m2)
        return tuple(m2s)

    def values(n, slot, m2s):
        q0 = n * blk if isinstance(n, int) else pl.multiple_of(n * blk, blk)
        oe = jnp.dot(p_s[slot], vp_s[pl.ds(q0, span), :], preferred_element_type=F32)
        for g in range(Q_GROUP):
            r = slice(g * blk, (g + 1) * blk)
            den = oe[r, HEAD_DIM:] + jnp.exp2(sink_ref[l, hk * Q_GROUP + g] * LOG2E - m2s[g])
            o_ref[0, pl.ds(q0, blk), g * HEAD_DIM:(g + 1) * HEAD_DIM] = (oe[r, :HEAD_DIM] / den).astype(BF16)

    logits(0, 0)
    p_s[1] = jnp.zeros(p_s.shape[1:], BF16)

    def body(i, m_prev):
        n = 2 * i
        logits(n + 1, 1)
        m_even = softmax(n, 0)
        values(jnp.maximum(n - 1, 0), 1, m_prev)
        logits(jnp.minimum(n + 2, nblk - 1), 0)
        m_odd = softmax(n + 1, 1)
        values(n, 0, m_even)
        return m_odd

    assert nblk % 2 == 0
    m_init = tuple(jnp.zeros((blk, 1), F32) for _ in range(Q_GROUP))
    m_last = lax.fori_loop(0, nblk // 2, body, m_init)
    values(nblk - 1, 1, m_last)


def _attention(qkv3, l, sink):
    nb, seq, _ = qkv3.shape
    gw = Q_GROUP * HEAD_DIM
    k_blk0 = N_HEADS
    v_blk0 = N_HEADS + N_KV_HEADS
    rows = Q_GROUP * WINDOW
    return pl.pallas_call(
        functools.partial(_attn_kernel, l=l),
        grid=(nb, N_KV_HEADS),
        in_specs=[pl.BlockSpec(memory_space=pltpu.SMEM),
                  pl.BlockSpec((1, seq, gw), lambda b, h: (b, 0, h)),
                  pl.BlockSpec((1, seq, HEAD_DIM), lambda b, h: (b, 0, k_blk0 + h)),
                  pl.BlockSpec((1, seq, HEAD_DIM), lambda b, h: (b, 0, v_blk0 + h))],
        out_specs=pl.BlockSpec((1, seq, gw), lambda b, h: (b, 0, h)),
        out_shape=jax.ShapeDtypeStruct((nb, seq, N_HEADS * HEAD_DIM), BF16),
        scratch_shapes=[pltpu.VMEM((seq // WINDOW + 2, HEAD_DIM, WINDOW), BF16),
                        pltpu.VMEM((seq + 2 * WINDOW, 2 * HEAD_DIM), BF16),
                        pltpu.VMEM((2, rows, 3 * WINDOW), F32),
                        pltpu.VMEM((2, rows, 3 * WINDOW), BF16)],
        compiler_params=_cparams("parallel", "parallel"),
        name="swa_attention",
    )(sink, qkv3, qkv3, qkv3)


def _merge_kernel(hb_ref, attn_ref, hf_ref, hbk_ref, h_ref,
                  wz_ref, wpa_ref, wpr_ref, wo_ref, g_ref, b_ref, o_ref, ob_ref, at_s, *, alpha):
    d = h_ref.shape[1]
    nb = attn_ref.shape[0]
    z = jnp.dot(hb_ref[...], wz_ref[...], preferred_element_type=F32)
    rnn = (hf_ref[...] + hbk_ref[...]) * _gelu(z[:, :d])
    for b in range(nb):
        _put_sequence(at_s, b, attn_ref[b].astype(F32))
    pa = jnp.dot(_slabs(at_s).astype(BF16), wpa_ref[...], preferred_element_type=F32)
    pr = jnp.dot(rnn.astype(BF16), wpr_ref[...], preferred_element_type=F32)
    merged = jax.nn.sigmoid(z[:, d:2 * d]) * pa + jax.nn.sigmoid(z[:, 2 * d:]) * pr
    o = jnp.dot(merged.astype(BF16), wo_ref[...], preferred_element_type=F32)
    y = _layer_norm(alpha * h_ref[...] + o, g_ref[...], b_ref[...])
    o_ref[...] = y
    ob_ref[...] = y.astype(BF16)


def _merge(hb, attn3, hf, hbk, h, l, wz, wpa, wpr, wo, g, b, alpha, tm):
    m, d = h.shape
    nb = attn3.shape[0]
    row = pl.BlockSpec((tm, d), lambda i: (i, 0))
    return pl.pallas_call(
        functools.partial(_merge_kernel, alpha=alpha),
        grid=(m // tm,),
        in_specs=[row, pl.BlockSpec((nb, tm // nb, d), lambda i: (0, i, 0)), row, row, row]
                 + [_layer_spec(a, l) for a in (wz, wpa, wpr, wo, g, b)],
        out_specs=[row, row],
        out_shape=[jax.ShapeDtypeStruct((m, d), F32), jax.ShapeDtypeStruct((m, d), BF16)],
        scratch_shapes=[_slab_scratch(tm, d)],
        compiler_params=_cparams("parallel"),
        name="merge_out_ln",
    )(hb, attn3, hf, hbk, h, wz, wpa, wpr, wo, g, b)


def _ffn_kernel(hb_ref, hprev_ref, hnext_ref, h_ref, p_ref, wu_ref, wd_ref, cw_ref, cb_ref,
                wpg_ref, wple_ref, g_ref, b_ref, *rest, alpha, tf, nb, batch_major_out):
    if batch_major_out:
        o_ref, p_s, y_s = rest
    else:
        o_ref, ob_ref, p_s = rest
    i = pl.program_id(0)
    tm = hb_ref.shape[0]
    dff = wd_ref.shape[0]
    halo = hprev_ref.shape[0]
    hb = hb_ref[...]
    hp = hprev_ref[...]
    hn = hnext_ref[...]
    hp = jnp.where(i == 0, jnp.zeros_like(hp), hp)
    hn = jnp.where(i == pl.num_programs(0) - 1, jnp.zeros_like(hn), hn)
    lhs = jnp.concatenate([hp, hb, hn], axis=0)
    acc = None
    for j in range(dff // tf):
        cs = slice(j * tf, (j + 1) * tf)
        gate = jnp.dot(lhs, wu_ref[:, cs], preferred_element_type=F32)
        val = jnp.dot(hb, wu_ref[:, dff + j * tf:dff + (j + 1) * tf], preferred_element_type=F32)
        conv = (cw_ref[0:1, cs] * gate[halo - nb:halo - nb + tm] + cw_ref[1:2, cs] * gate[halo:halo + tm]
                + cw_ref[2:3, cs] * gate[halo + nb:halo + nb + tm] + cb_ref[:, cs])
        act = (_gelu(conv) * val).astype(BF16)
        pc = jnp.dot(act, wd_ref[cs, :], preferred_element_type=F32)
        acc = pc if acc is None else acc + pc

    for b in range(nb):
        _put_sequence(p_s, b, p_ref[b])
    ple = (jax.nn.sigmoid(jnp.dot(hb, wpg_ref[...], preferred_element_type=F32))
           * jnp.dot(_slabs(p_s).astype(BF16), wple_ref[...], preferred_element_type=F32))
    y = _layer_norm(alpha * h_ref[...] + acc + ple, g_ref[...], b_ref[...])
    if batch_major_out:
        for c in range(y_s.shape[0]):
            y_s[c] = y[:, c * LANES:(c + 1) * LANES]
        for b in range(nb):
            for c in range(y_s.shape[0]):
                o_ref[b, :, c * LANES:(c + 1) * LANES] = _get_sequence(y_s, b, tm // nb, c)
    else:
        o_ref[...] = y
        ob_ref[...] = y.astype(BF16)


def _ffn(h, hb, l, p4, w_up, w_down, cw, cb, wpg, wple, g, b, alpha, tm, tf, batch_major_out):
    m, d = h.shape
    _, nb, seq, dple = p4.shape
    tt = tm // nb
    halo = BF16_ROWS
    per = tm // halo
    last = m // halo - 1
    row = pl.BlockSpec((tm, d), lambda i: (i, 0))
    if batch_major_out:
        out_specs = pl.BlockSpec((nb, tt, d), lambda i: (0, i, 0))
        out_shape = jax.ShapeDtypeStruct((nb, seq, d), F32)
        scratch = [_slab_scratch(tm, dple), _slab_scratch(tm, d)]
    else:
        out_specs = [row, row]
        out_shape = [jax.ShapeDtypeStruct((m, d), F32), jax.ShapeDtypeStruct((m, d), BF16)]
        scratch = [_slab_scratch(tm, dple)]
    return pl.pallas_call(
        functools.partial(_ffn_kernel, alpha=alpha, tf=tf, nb=nb, batch_major_out=batch_major_out),
        grid=(m // tm,),
        in_specs=[row,
                  pl.BlockSpec((halo, d), lambda i: (jnp.maximum(i * per - 1, 0), 0)),
                  pl.BlockSpec((halo, d), lambda i: (jnp.minimum((i + 1) * per, last), 0)),
                  row,
                  pl.BlockSpec((None, nb, tt, dple), lambda i: (l, 0, i, 0))]
                 + [_layer_spec(a, l) for a in (w_up, w_down, cw, cb, wpg, wple, g, b)],
        out_specs=out_specs,
        out_shape=out_shape,
        scratch_shapes=scratch,
        compiler_params=_cparams("parallel"),
        name="ffn_ple_ln",
    )(hb, hb, hb, h, p4, w_up, w_down, cw, cb, wpg, wple, g, b)


def _rope_tables(seq):
    half = ROPE_DIM // 2
    mid = HEAD_DIM // 2
    pos = jnp.arange(seq, dtype=F32)
    inv = ROPE_THETA ** (-jnp.arange(0, ROPE_DIM, 2, dtype=F32) / ROPE_DIM)
    ang = pos[:, None] * inv[None, :]
    cos, sin = jnp.cos(ang), jnp.sin(ang)
    ones = jnp.ones((seq, mid - half), F32)
    zeros = jnp.zeros((seq, mid - half), F32)
    c = jnp.concatenate([cos, ones, cos, ones], axis=1)
    s = jnp.concatenate([-sin, zeros, sin, zeros], axis=1)
    return c, s


def kernel(x, p, ln_in_g, ln_in_b, w_in, attn_sink, rnn_conv_w, rnn_conv_b, rg_w_a, rg_b_a, rg_w_x, rg_b_x, rg_lambda, w_proj_attn, w_proj_rnn, w_out, ln1_g, ln1_b, w_up, ffn_conv_w, ffn_conv_b, w_down, w_ple, w_ple_gate, ln2_g, ln2_b):
    nb, seq, d = x.shape
    depth = w_in.shape[0]
    m = nb * seq
    alpha = float((2 * depth) ** 0.25)
    n_qkv = (N_HEADS + 2 * N_KV_HEADS) * HEAD_DIM
    assert nb == SUBLANES and d == N_HEADS * HEAD_DIM and seq % WINDOW == 0 and seq >= 3 * WINDOW

    tm_ln = min(TM_LN, m)
    tm_proj = min(TM_PROJ, m)
    tm_merge = min(TM_MERGE, m)
    tm_ffn = min(TM_FFN, m)
    ts_rnn = min(TS_RNN, seq)

    w_a, w_z = _w_in_cast(w_in, n_qkv + d, min(TR_CAST, d))
    w_g = jnp.concatenate([rg_w_a, rg_w_x], axis=-1).astype(BF16)
    w_pa, w_pr, w_o = w_proj_attn.astype(BF16), w_proj_rnn.astype(BF16), w_out.astype(BF16)
    w_u, w_d = w_up.astype(BF16), w_down.astype(BF16)
    w_pg, w_pl = w_ple_gate.astype(BF16), w_ple.astype(BF16)

    row = lambda a: a.reshape(a.shape[:-1] + (1, a.shape[-1]))
    conv_w, conv_b = row(rnn_conv_w), row(rnn_conv_b)
    b_a, b_x, lam = row(rg_b_a), row(rg_b_x), row(rg_lambda)
    g1, b1, g2, b2, ffn_cb = row(ln1_g), row(ln1_b), row(ln2_g), row(ln2_b), row(ffn_conv_b)

    rope_c, rope_s = _rope_tables(seq)
    h, hb = _ln_in(x, ln_in_g, ln_in_b, tm_ln)

    for l in range(depth):
        qkv, xr = _in_proj(hb, l, w_a, rope_c, rope_s, nb, n_qkv, tm_proj, TN)
        hf, hbk = _rnn(xr, l, conv_w, conv_b, w_g, b_a, b_x, lam, nb, ts_rnn)
        attn = _attention(qkv, l, attn_sink)
        h, hb = _merge(hb, attn, hf, hbk, h, l, w_z, w_pa, w_pr, w_o, g1, b1, alpha, tm_merge)
        out = _ffn(h, hb, l, p, w_u, w_d, ffn_conv_w, ffn_cb, w_pg, w_pl, g2, b2,
                   alpha, tm_ffn, TF_FFN, batch_major_out=(l == depth - 1))
        if l < depth - 1:
            h, hb = out
    return out
```

```python
import functools
import math

import jax
import jax.numpy as jnp
from jax import lax
from jax.experimental import pallas as pl
from jax.experimental.pallas import tpu as pltpu

F32 = jnp.float32
BF16 = jnp.bfloat16

N_HEADS = 8
N_KV_HEADS = 2
HEAD_DIM = 128
Q_GROUP = N_HEADS // N_KV_HEADS
WINDOW = 128
ROPE_THETA = 500000.0
ROPE_DIM = HEAD_DIM // 4
N_RNN_BLOCKS = 4
RNN_CONV_W = 4
FFN_CONV_W = 3
RG_C = 8.0
LN_EPS = 1e-5
NEG_INF = -1e30
LOG2E = math.log2(math.e)
GELU_C = math.sqrt(2.0 / math.pi)
GELU_A = 0.044715

LANES = 128
SUBLANES = 8
BF16_ROWS = 16
MXU_COLS = 256
VMEM_LIMIT = 56 * 1024 * 1024

TM_LN = 2048
TM_PROJ = 2048
TM_MERGE = 512
TM_FFN = 512
TS_RNN = 64
RNN_BLOCKS_PER_STEP = 4
TN = 512
TF_FFN = 1536
TR_CAST = 256


def _cparams(*sem):
    return pltpu.CompilerParams(dimension_semantics=sem, vmem_limit_bytes=VMEM_LIMIT)


def _layer_norm(x, g, b):
    mu = jnp.mean(x, axis=-1, keepdims=True)
    xc = x - mu
    var = jnp.mean(xc * xc, axis=-1, keepdims=True)
    return xc * lax.rsqrt(var + LN_EPS) * g + b


def _gelu(x):
    k1 = -2.0 * GELU_C * LOG2E
    return x / (1.0 + jnp.exp2(x * (k1 + (k1 * GELU_A) * (x * x))))


def _softplus(x):
    return jnp.maximum(x, 0.0) + jnp.log1p(jnp.exp(-jnp.abs(x)))


def _const_spec(shape):
    nd = len(shape)
    return pl.BlockSpec(shape, lambda *_: (0,) * nd, pipeline_mode=pl.Buffered(1))


def _layer_spec(arr, l):
    tail = (0,) * (arr.ndim - 1)
    return pl.BlockSpec((None,) + arr.shape[1:], lambda *_: (l,) + tail, pipeline_mode=pl.Buffered(1))


def _slab_scratch(rows, cols):
    return pltpu.VMEM((cols // LANES, rows, LANES), F32)


def _put_sequence(slab_s, b, val):
    tt, n = val.shape
    for c in range(n // LANES):
        slab_s[c, pl.ds(b, tt, stride=SUBLANES), :] = val[:, c * LANES:(c + 1) * LANES]


def _get_sequence(slab_s, b, tt, c):
    return slab_s[c, pl.ds(b, tt, stride=SUBLANES), :]


def _slabs(slab_s):
    return jnp.concatenate([slab_s[c] for c in range(slab_s.shape[0])], axis=1)


def _ln_in_kernel(x_ref, g_ref, b_ref, h_ref, hb_ref, t_s):
    nb = x_ref.shape[0]
    for b in range(nb):
        _put_sequence(t_s, b, _layer_norm(x_ref[b], g_ref[...], b_ref[...]))
    y = _slabs(t_s)
    h_ref[...] = y
    hb_ref[...] = y.astype(BF16)


def _ln_in(x, g, b, tm):
    nb, seq, d = x.shape
    m = nb * seq
    tt = tm // nb
    return pl.pallas_call(
        _ln_in_kernel,
        grid=(m // tm,),
        in_specs=[pl.BlockSpec((nb, tt, d), lambda i: (0, i, 0)),
                  _const_spec((1, d)), _const_spec((1, d))],
        out_specs=[pl.BlockSpec((tm, d), lambda i: (i, 0)),
                   pl.BlockSpec((tm, d), lambda i: (i, 0))],
        out_shape=[jax.ShapeDtypeStruct((m, d), F32), jax.ShapeDtypeStruct((m, d), BF16)],
        scratch_shapes=[_slab_scratch(tm, d)],
        compiler_params=_cparams("parallel"),
        name="ln_in",
    )(x, g.reshape(1, d), b.reshape(1, d))


def _w_in_cast_kernel(w_ref, wa_ref, wz_ref):
    half = ROPE_DIM // 2
    mid = HEAD_DIM // 2
    pieces = ((0, half), (ROPE_DIM, mid + half), (half, ROPE_DIM), (mid + half, HEAD_DIM))
    na = wa_ref.shape[1]
    for hd in range(N_HEADS + N_KV_HEADS):
        x = w_ref[:, hd * HEAD_DIM:(hd + 1) * HEAD_DIM]
        y = jnp.concatenate([x[:, a:b] for a, b in pieces], axis=1)
        wa_ref[:, hd * HEAD_DIM:(hd + 1) * HEAD_DIM] = y.astype(BF16)
    rest = (N_HEADS + N_KV_HEADS) * HEAD_DIM
    wa_ref[:, rest:] = w_ref[:, rest:na].astype(BF16)
    wz_ref[...] = w_ref[:, na:].astype(BF16)


def _w_in_cast(w_in, na, tr):
    depth, d, n = w_in.shape
    return pl.pallas_call(
        _w_in_cast_kernel,
        grid=(depth, d // tr),
        in_specs=[pl.BlockSpec((None, tr, n), lambda l, i: (l, i, 0))],
        out_specs=[pl.BlockSpec((None, tr, na), lambda l, i: (l, i, 0)),
                   pl.BlockSpec((None, tr, n - na), lambda l, i: (l, i, 0))],
        out_shape=[jax.ShapeDtypeStruct((depth, d, na), BF16),
                   jax.ShapeDtypeStruct((depth, d, n - na), BF16)],
        compiler_params=_cparams("parallel", "parallel"),
        name="w_in_cast",
    )(w_in)


def _in_proj_kernel(h_ref, w_ref, c_ref, s_ref, o_ref, xr_ref, *z_scratch, n_qkv_tiles, n_rope_last):
    step = pl.program_id(1)
    j = step // 2
    is_xr = step % 2 == 1
    last = n_qkv_tiles - 1
    nb, tt, _ = o_ref.shape
    h = h_ref[...]

    @pl.when(is_xr)
    def _():
        xr_ref[...] = jnp.dot(h, w_ref[...], preferred_element_type=F32)

    @pl.when(jnp.logical_not(is_xr))
    def _():
        c, s = c_ref[...], s_ref[...]

        def rope(zc):
            return zc * c + pltpu.roll(zc, LANES // 2, 1) * s

        heads_per_dot = MXU_COLS // LANES

        def emit(z_s, ch, cc, rotary):
            for b in range(nb):
                zc = _get_sequence(z_s, b, tt, ch)
                o_ref[b, :, cc * LANES:(cc + 1) * LANES] = (rope(zc) if rotary else zc).astype(BF16)

        for cd, z_s in enumerate(z_scratch):
            z = jnp.dot(h, w_ref[:, cd * MXU_COLS:(cd + 1) * MXU_COLS], preferred_element_type=F32)
            for ch in range(heads_per_dot):
                z_s[ch] = z[:, ch * LANES:(ch + 1) * LANES]
            for ch in range(heads_per_dot):
                cc = cd * heads_per_dot + ch
                if cc < n_rope_last:
                    emit(z_s, ch, cc, True)
                else:
                    pl.when(j < last)(functools.partial(emit, z_s, ch, cc, True))
                    pl.when(j == last)(functools.partial(emit, z_s, ch, cc, False))


def _in_proj(hb, l, w, rope_c, rope_s, nb, n_qkv, tm, tn):
    m, d = hb.shape
    n = w.shape[2]
    seq = m // nb
    tt = tm // nb
    n_rope_cols = (N_HEADS + N_KV_HEADS) * HEAD_DIM
    assert n % tn == 0 and n_qkv % tn == 0 and tn % MXU_COLS == 0 and (n_qkv - tn) <= n_rope_cols <= n_qkv
    nq = n_qkv // tn
    assert (n - n_qkv) // tn == nq - 1
    n_rope_last = (n_rope_cols - (n_qkv - tn)) // LANES
    tab = pl.BlockSpec((tt, LANES), lambda i, j: (i, 0))
    return pl.pallas_call(
        functools.partial(_in_proj_kernel, n_qkv_tiles=nq, n_rope_last=n_rope_last),
        grid=(m // tm, n // tn),
        in_specs=[pl.BlockSpec((tm, d), lambda i, s: (i, 0)),
                  pl.BlockSpec((None, d, tn), lambda i, s: (l, 0, s // 2 + (s % 2) * nq)),
                  tab, tab],
        out_specs=[pl.BlockSpec((nb, tt, tn), lambda i, s: (0, i, s // 2)),
                   pl.BlockSpec((tm, tn), lambda i, s: (i, jnp.maximum(s - 1, 0) // 2))],
        out_shape=[jax.ShapeDtypeStruct((nb, seq, n_qkv), BF16),
                   jax.ShapeDtypeStruct((m, n - n_qkv), F32)],
        scratch_shapes=[_slab_scratch(tm, MXU_COLS)] * (tn // MXU_COLS),
        compiler_params=_cparams("parallel", "arbitrary"),
        name="in_proj",
    )(hb, w, rope_c, rope_s)


def _rnn_kernel(xf_ref, xb_ref, cw_ref, cb_ref, wg_ref, ba_ref, bx_ref, lam_ref,
                hf_ref, hb_ref, xsf_s, xsb_s, af_s, uf_s, ab_s, ub_s, carry_s, *, nb):
    i = pl.program_id(1)
    rows, bw = hf_ref.shape
    blk = wg_ref.shape[2]
    ts = rows // nb
    halo = (RNN_CONV_W - 1) * nb

    @pl.when(i == 0)
    def _():
        xsf_s[0:halo, :] = jnp.zeros((halo, bw), F32)
        xsb_s[rows:rows + halo, :] = jnp.zeros((halo, bw), F32)
        carry_s[...] = jnp.zeros_like(carry_s)

    def gates(d, half_x, start, a_s, u_s):
        xcb = half_x.astype(BF16)
        gs = [jnp.dot(xcb[:, sb * blk:(sb + 1) * blk], wg_ref[d, sb], preferred_element_type=F32)
              for sb in range(bw // blk)]
        g_a = jnp.concatenate([g[:, :blk] for g in gs], axis=1)
        g_x = jnp.concatenate([g[:, blk:] for g in gs], axis=1)
        t_a = jnp.tanh(g_a + ba_ref[d] * 0.5)
        t_x = jnp.tanh(g_x + bx_ref[d] * 0.5)
        half_c = (-0.5 * RG_C) * _softplus(-lam_ref[d])
        log_a = half_c + half_c * t_a
        a = jnp.exp(log_a)
        w = jnp.tanh(log_a) * (-1.0 - a * a)
        mult = jnp.where(w > 0.0, w * lax.rsqrt(w), 0.0)
        xg = half_x + half_x * t_x
        a_s[...] = a
        u_s[...] = xg * mult
        return xg[start, :]

    xsf_s[halo:halo + rows, :] = xf_ref[...]
    yf = cb_ref[0] * 0.5 + (cw_ref[0, 0] * 0.5) * xsf_s[halo:halo + rows, :]
    for k in range(1, RNN_CONV_W):
        yf = yf + (cw_ref[0, k] * 0.5) * xsf_s[halo - k * nb:halo - k * nb + rows, :]
    xsf_s[0:halo, :] = xsf_s[rows:rows + halo, :]
    first_f = gates(0, yf, slice(0, nb), af_s, uf_s)

    xsb_s[0:rows, :] = xb_ref[...]
    yb = cb_ref[1] * 0.5 + (cw_ref[1, 0] * 0.5) * xsb_s[0:rows, :]
    for k in range(1, RNN_CONV_W):
        yb = yb + (cw_ref[1, k] * 0.5) * xsb_s[k * nb:k * nb + rows, :]
    xsb_s[rows:rows + halo, :] = xsb_s[0:halo, :]
    first_b = gates(1, yb, slice(rows - nb, rows), ab_s, ub_s)

    @pl.when(i == 0)
    def _():
        uf_s[0:nb, :] = first_f
        ub_s[rows - nb:rows, :] = first_b

    def step(tt, carry):
        cf, cb = carry
        rf = pl.ds(pl.multiple_of(tt * nb, nb), nb)
        rb = pl.ds(pl.multiple_of((ts - 1 - tt) * nb, nb), nb)
        cf = af_s[rf, :] * cf + uf_s[rf, :]
        cb = ab_s[rb, :] * cb + ub_s[rb, :]
        hf_ref[rf, :] = cf
        hb_ref[rb, :] = cb
        return cf, cb

    cf, cb = lax.fori_loop(0, ts, step, (carry_s[0], carry_s[1]), unroll=8)
    carry_s[0] = cf
    carry_s[1] = cb


def _rnn(xr, l, conv_w, conv_b, wg, b_a, b_x, lam, nb, ts):
    m, d = xr.shape
    blk = d // N_RNN_BLOCKS
    bw = RNN_BLOCKS_PER_STEP * blk
    rows = ts * nb
    nt = m // rows
    halo = (RNN_CONV_W - 1) * nb
    xs = pltpu.VMEM((rows + halo, bw), F32)
    au = pltpu.VMEM((rows, bw), F32)
    vec = lambda: pl.BlockSpec((None, 2, 1, bw), lambda c, i: (l, 0, 0, c))
    return pl.pallas_call(
        functools.partial(_rnn_kernel, nb=nb),
        grid=(d // bw, nt),
        in_specs=[pl.BlockSpec((rows, bw), lambda c, i: (i, c)),
                  pl.BlockSpec((rows, bw), lambda c, i: (nt - 1 - i, c)),
                  pl.BlockSpec((None, 2, RNN_CONV_W, 1, bw), lambda c, i: (l, 0, 0, 0, c)),
                  vec(),
                  pl.BlockSpec((None, 2, RNN_BLOCKS_PER_STEP, blk, 2 * blk), lambda c, i: (l, 0, c, 0, 0)),
                  vec(), vec(), vec()],
        out_specs=[pl.BlockSpec((rows, bw), lambda c, i: (i, c)),
                   pl.BlockSpec((rows, bw), lambda c, i: (nt - 1 - i, c))],
        out_shape=[jax.ShapeDtypeStruct((m, d), F32)] * 2,
        scratch_shapes=[xs, xs, au, au, au, au, pltpu.VMEM((2, nb, bw), F32)],
        compiler_params=_cparams("parallel", "arbitrary"),
        name="rg_lru",
    )(xr, xr, conv_w, conv_b, wg, b_a, b_x, lam)


def _attn_kernel(sink_ref, q_ref, k_ref, v_ref, o_ref, kp_s, vp_s, s_s, p_s, *, l):
    hk = pl.program_id(1)
    seq = q_ref.shape[1]
    blk = WINDOW
    nblk = seq // blk
    span = 3 * blk
    c1 = HEAD_DIM ** -0.5 * LOG2E

    kp_s[0] = jnp.zeros((HEAD_DIM, blk), BF16)
    for jb in range(nblk):
        kp_s[jb + 1] = k_ref[0, jb * blk:(jb + 1) * blk, :].T
    kp_s[nblk + 1] = jnp.zeros((HEAD_DIM, blk), BF16)
    vp_s[0:blk, 0:HEAD_DIM] = jnp.zeros((blk, HEAD_DIM), BF16)
    vp_s[blk:blk + seq, 0:HEAD_DIM] = v_ref[0]
    vp_s[blk + seq:2 * blk + seq, 0:HEAD_DIM] = jnp.zeros((blk, HEAD_DIM), BF16)
    vp_s[:, HEAD_DIM:] = jnp.ones((seq + 2 * blk, HEAD_DIM), BF16)

    qi = lax.broadcasted_iota(jnp.int32, (blk, blk), 0)
    kj = lax.broadcasted_iota(jnp.int32, (blk, blk), 1)
    bias_prev = jnp.where(kj >= qi, 0.0, NEG_INF)
    bias_next = jnp.where(kj <= qi, 0.0, NEG_INF)

    def logits(n, slot):
        q0 = pl.multiple_of(n * blk, blk)
        qs = jnp.concatenate(
            [q_ref[0, pl.ds(q0, blk), g * HEAD_DIM:(g + 1) * HEAD_DIM] for g in range(Q_GROUP)], axis=0)
        kw = jnp.concatenate([kp_s[n], kp_s[n + 1], kp_s[n + 2]], axis=1)
        s_s[slot] = jnp.dot(qs, kw, preferred_element_type=F32)

    def softmax(n, slot):
        bp = jnp.where(n == 0, NEG_INF, bias_prev)
        bn = jnp.where(n == nblk - 1, NEG_INF, bias_next)
        m2s = []
        for g in range(Q_GROUP):
            r = slice(g * blk, (g + 1) * blk)
            s_prev = s_s[slot, r, 0:blk] + bp
            s_cur = s_s[slot, r, blk:2 * blk]
            s_next = s_s[slot, r, 2 * blk:span] + bn
            m_raw = jnp.max(jnp.maximum(jnp.maximum(s_prev, s_cur), s_next), axis=-1, keepdims=True)
            m2 = jnp.maximum(m_raw * c1, sink_ref[l, hk * Q_GROUP + g] * LOG2E)
            p_s[slot, r, 0:blk] = jnp.exp2(s_prev * c1 - m2).astype(BF16)
            p_s[slot, r, blk:2 * blk] = jnp.exp2(s_cur * c1 - m2).astype(BF16)
            p_s[slot, r, 2 * blk:span] = jnp.exp2(s_next * c1 - m2).astype(BF16)
            m2s.append(m2)
        return tuple(m2s)

    def values(n, slot, m2s):
        q0 = n * blk if isinstance(n, int) else pl.multiple_of(n * blk, blk)
        oe = jnp.dot(p_s[slot], vp_s[pl.ds(q0, span), :], preferred_element_type=F32)
        for g in range(Q_GROUP):
            r = slice(g * blk, (g + 1) * blk)
            den = oe[r, HEAD_DIM:] + jnp.exp2(sink_ref[l, hk * Q_GROUP + g] * LOG2E - m2s[g])
            o_ref[0, pl.ds(q0, blk), g * HEAD_DIM:(g + 1) * HEAD_DIM] = (oe[r, :HEAD_DIM] / den).astype(BF16)

    logits(0, 0)
    p_s[1] = jnp.zeros(p_s.shape[1:], BF16)

    def body(i, m_prev):
        n = 2 * i
        logits(n + 1, 1)
        m_even = softmax(n, 0)
        values(jnp.maximum(n - 1, 0), 1, m_prev)
        logits(jnp.minimum(n + 2, nblk - 1), 0)
        m_odd = softmax(n + 1, 1)
        values(n, 0, m_even)
        return m_odd

    assert nblk % 2 == 0
    m_init = tuple(jnp.zeros((blk, 1), F32) for _ in range(Q_GROUP))
    m_last = lax.fori_loop(0, nblk // 2, body, m_init)
    values(nblk - 1, 1, m_last)


def _attention(qkv3, l, sink):
    nb, seq, _ = qkv3.shape
    gw = Q_GROUP * HEAD_DIM
    k_blk0 = N_HEADS
    v_blk0 = N_HEADS + N_KV_HEADS
    rows = Q_GROUP * WINDOW
    return pl.pallas_call(
        functools.partial(_attn_kernel, l=l),
        grid=(nb, N_KV_HEADS),
        in_specs=[pl.BlockSpec(memory_space=pltpu.SMEM),
                  pl.BlockSpec((1, seq, gw), lambda b, h: (b, 0, h)),
                  pl.BlockSpec((1, seq, HEAD_DIM), lambda b, h: (b, 0, k_blk0 + h)),
                  pl.BlockSpec((1, seq, HEAD_DIM), lambda b, h: (b, 0, v_blk0 + h))],
        out_specs=pl.BlockSpec((1, seq, gw), lambda b, h: (b, 0, h)),
        out_shape=jax.ShapeDtypeStruct((nb, seq, N_HEADS * HEAD_DIM), BF16),
        scratch_shapes=[pltpu.VMEM((seq // WINDOW + 2, HEAD_DIM, WINDOW), BF16),
                        pltpu.VMEM((seq + 2 * WINDOW, 2 * HEAD_DIM), BF16),
                        pltpu.VMEM((2, rows, 3 * WINDOW), F32),
                        pltpu.VMEM((2, rows, 3 * WINDOW), BF16)],
        compiler_params=_cparams("parallel", "parallel"),
        name="swa_attention",
    )(sink, qkv3, qkv3, qkv3)


def _merge_kernel(hb_ref, attn_ref, hf_ref, hbk_ref, h_ref,
                  wz_ref, wpa_ref, wpr_ref, wo_ref, g_ref, b_ref, o_ref, ob_ref, at_s, *, alpha):
    d = h_ref.shape[1]
    nb = attn_ref.shape[0]
    z = jnp.dot(hb_ref[...], wz_ref[...], preferred_element_type=F32)
    rnn = (hf_ref[...] + hbk_ref[...]) * _gelu(z[:, :d])
    for b in range(nb):
        _put_sequence(at_s, b, attn_ref[b].astype(F32))
    pa = jnp.dot(_slabs(at_s).astype(BF16), wpa_ref[...], preferred_element_type=F32)
    pr = jnp.dot(rnn.astype(BF16), wpr_ref[...], preferred_element_type=F32)
    merged = jax.nn.sigmoid(z[:, d:2 * d]) * pa + jax.nn.sigmoid(z[:, 2 * d:]) * pr
    o = jnp.dot(merged.astype(BF16), wo_ref[...], preferred_element_type=F32)
    y = _layer_norm(alpha * h_ref[...] + o, g_ref[...], b_ref[...])
    o_ref[...] = y
    ob_ref[...] = y.astype(BF16)


def _merge(hb, attn3, hf, hbk, h, l, wz, wpa, wpr, wo, g, b, alpha, tm):
    m, d = h.shape
    nb = attn3.shape[0]
    row = pl.BlockSpec((tm, d), lambda i: (i, 0))
    return pl.pallas_call(
        functools.partial(_merge_kernel, alpha=alpha),
        grid=(m // tm,),
        in_specs=[row, pl.BlockSpec((nb, tm // nb, d), lambda i: (0, i, 0)), row, row, row]
                 + [_layer_spec(a, l) for a in (wz, wpa, wpr, wo, g, b)],
        out_specs=[row, row],
        out_shape=[jax.ShapeDtypeStruct((m, d), F32), jax.ShapeDtypeStruct((m, d), BF16)],
        scratch_shapes=[_slab_scratch(tm, d)],
        compiler_params=_cparams("parallel"),
        name="merge_out_ln",
    )(hb, attn3, hf, hbk, h, wz, wpa, wpr, wo, g, b)


def _ffn_kernel(hb_ref, hprev_ref, hnext_ref, h_ref, p_ref, wu_ref, wd_ref, cw_ref, cb_ref,
                wpg_ref, wple_ref, g_ref, b_ref, *rest, alpha, tf, nb, batch_major_out):
    if batch_major_out:
        o_ref, p_s, y_s = rest
    else:
        o_ref, ob_ref, p_s = rest
    i = pl.program_id(0)
    tm = hb_ref.shape[0]
    dff = wd_ref.shape[0]
    halo = hprev_ref.shape[0]
    hb = hb_ref[...]
    hp = hprev_ref[...]
    hn = hnext_ref[...]
    hp = jnp.where(i == 0, jnp.zeros_like(hp), hp)
    hn = jnp.where(i == pl.num_programs(0) - 1, jnp.zeros_like(hn), hn)
    lhs = jnp.concatenate([hp, hb, hn], axis=0)
    acc = None
    for j in range(dff // tf):
        cs = slice(j * tf, (j + 1) * tf)
        gate = jnp.dot(lhs, wu_ref[:, cs], preferred_element_type=F32)
        val = jnp.dot(hb, wu_ref[:, dff + j * tf:dff + (j + 1) * tf], preferred_element_type=F32)
        conv = (cw_ref[0:1, cs] * gate[halo - nb:halo - nb + tm] + cw_ref[1:2, cs] * gate[halo:halo + tm]
                + cw_ref[2:3, cs] * gate[halo + nb:halo + nb + tm] + cb_ref[:, cs])
        act = (_gelu(conv) * val).astype(BF16)
        pc = jnp.dot(act, wd_ref[cs, :], preferred_element_type=F32)
        acc = pc if acc is None else acc + pc

    for b in range(nb):
        _put_sequence(p_s, b, p_ref[b])
    ple = (jax.nn.sigmoid(jnp.dot(hb, wpg_ref[...], preferred_element_type=F32))
           * jnp.dot(_slabs(p_s).astype(BF16), wple_ref[...], preferred_element_type=F32))
    y = _layer_norm(alpha * h_ref[...] + acc + ple, g_ref[...], b_ref[...])
    if batch_major_out:
        for c in range(y_s.shape[0]):
            y_s[c] = y[:, c * LANES:(c + 1) * LANES]
        for b in range(nb):
            for c in range(y_s.shape[0]):
                o_ref[b, :, c * LANES:(c + 1) * LANES] = _get_sequence(y_s, b, tm // nb, c)
    else:
        o_ref[...] = y
        ob_ref[...] = y.astype(BF16)


def _ffn(h, hb, l, p4, w_up, w_down, cw, cb, wpg, wple, g, b, alpha, tm, tf, batch_major_out):
    m, d = h.shape
    _, nb, seq, dple = p4.shape
    tt = tm // nb
    halo = BF16_ROWS
    per = tm // halo
    last = m // halo - 1
    row = pl.BlockSpec((tm, d), lambda i: (i, 0))
    if batch_major_out:
        out_specs = pl.BlockSpec((nb, tt, d), lambda i: (0, i, 0))
        out_shape = jax.ShapeDtypeStruct((nb, seq, d), F32)
        scratch = [_slab_scratch(tm, dple), _slab_scratch(tm, d)]
    else:
        out_specs = [row, row]
        out_shape = [jax.ShapeDtypeStruct((m, d), F32), jax.ShapeDtypeStruct((m, d), BF16)]
        scratch = [_slab_scratch(tm, dple)]
    return pl.pallas_call(
        functools.partial(_ffn_kernel, alpha=alpha, tf=tf, nb=nb, batch_major_out=batch_major_out),
        grid=(m // tm,),
        in_specs=[row,
                  pl.BlockSpec((halo, d), lambda i: (jnp.maximum(i * per - 1, 0), 0)),
                  pl.BlockSpec((halo, d), lambda i: (jnp.minimum((i + 1) * per, last), 0)),
                  row,
                  pl.BlockSpec((None, nb, tt, dple), lambda i: (l, 0, i, 0))]
                 + [_layer_spec(a, l) for a in (w_up, w_down, cw, cb, wpg, wple, g, b)],
        out_specs=out_specs,
        out_shape=out_shape,
        scratch_shapes=scratch,
        compiler_params=_cparams("parallel"),
        name="ffn_ple_ln",
    )(hb, hb, hb, h, p4, w_up, w_down, cw, cb, wpg, wple, g, b)


def _rope_tables(seq):
    half = ROPE_DIM // 2
    mid = HEAD_DIM // 2
    pos = jnp.arange(seq, dtype=F32)
    inv = ROPE_THETA ** (-jnp.arange(0, ROPE_DIM, 2, dtype=F32) / ROPE_DIM)
    ang = pos[:, None] * inv[None, :]
    cos, sin = jnp.cos(ang), jnp.sin(ang)
    ones = jnp.ones((seq, mid - half), F32)
    zeros = jnp.zeros((seq, mid - half), F32)
    c = jnp.concatenate([cos, ones, cos, ones], axis=1)
    s = jnp.concatenate([-sin, zeros, sin, zeros], axis=1)
    return c, s


def kernel(x, p, ln_in_g, ln_in_b, w_in, attn_sink, rnn_conv_w, rnn_conv_b, rg_w_a, rg_b_a, rg_w_x, rg_b_x, rg_lambda, w_proj_attn, w_proj_rnn, w_out, ln1_g, ln1_b, w_up, ffn_conv_w, ffn_conv_b, w_down, w_ple, w_ple_gate, ln2_g, ln2_b):
    nb, seq, d = x.shape
    depth = w_in.shape[0]
    m = nb * seq
    alpha = float((2 * depth) ** 0.25)
    n_qkv = (N_HEADS + 2 * N_KV_HEADS) * HEAD_DIM
    assert nb == SUBLANES and d == N_HEADS * HEAD_DIM and seq % WINDOW == 0 and seq >= 3 * WINDOW

    tm_ln = min(TM_LN, m)
    tm_proj = min(TM_PROJ, m)
    tm_merge = min(TM_MERGE, m)
    tm_ffn = min(TM_FFN, m)
    ts_rnn = min(TS_RNN, seq)

    w_a, w_z = _w_in_cast(w_in, n_qkv + d, min(TR_CAST, d))
    w_g = jnp.concatenate([rg_w_a, rg_w_x], axis=-1).astype(BF16)
    w_pa, w_pr, w_o = w_proj_attn.astype(BF16), w_proj_rnn.astype(BF16), w_out.astype(BF16)
    w_u, w_d = w_up.astype(BF16), w_down.astype(BF16)
    w_pg, w_pl = w_ple_gate.astype(BF16), w_ple.astype(BF16)

    row = lambda a: a.reshape(a.shape[:-1] + (1, a.shape[-1]))
    conv_w, conv_b = row(rnn_conv_w), row(rnn_conv_b)
    b_a, b_x, lam = row(rg_b_a), row(rg_b_x), row(rg_lambda)
    g1, b1, g2, b2, ffn_cb = row(ln1_g), row(ln1_b), row(ln2_g), row(ln2_b), row(ffn_conv_b)

    rope_c, rope_s = _rope_tables(seq)
    h, hb = _ln_in(x, ln_in_g, ln_in_b, tm_ln)

    for l in range(depth):
        qkv, xr = _in_proj(hb, l, w_a, rope_c, rope_s, nb, n_qkv, tm_proj, TN)
        hf, hbk = _rnn(xr, l, conv_w, conv_b, w_g, b_a, b_x, lam, nb, ts_rnn)
        attn = _attention(qkv, l, attn_sink)
        h, hb = _merge(hb, attn, hf, hbk, h, l, w_z, w_pa, w_pr, w_o, g1, b1, alpha, tm_merge)
        out = _ffn(h, hb, l, p, w_u, w_d, ffn_conv_w, ffn_cb, w_pg, w_pl, g2, b2,
                   alpha, tm_ffn, TF_FFN, batch_major_out=(l == depth - 1))
        if l < depth - 1:
            h, hb = out
    return out
```

```python
import functools
import math

import jax
import jax.numpy as jnp
from jax import lax
from jax.experimental import pallas as pl
from jax.experimental.pallas import tpu as pltpu

F32 = jnp.float32
BF16 = jnp.bfloat16

N_HEADS = 8
N_KV_HEADS = 2
HEAD_DIM = 128
Q_GROUP = N_HEADS // N_KV_HEADS
WINDOW = 128
ROPE_THETA = 500000.0
ROPE_DIM = HEAD_DIM // 4
N_RNN_BLOCKS = 4
RNN_CONV_W = 4
FFN_CONV_W = 3
RG_C = 8.0
LN_EPS = 1e-5
NEG_INF = -1e30
LOG2E = math.log2(math.e)
GELU_C = math.sqrt(2.0 / math.pi)
GELU_A = 0.044715

LANES = 128
SUBLANES = 8
BF16_ROWS = 16
MXU_COLS = 256
VMEM_LIMIT = 56 * 1024 * 1024

TM_LN = 2048
TM_PROJ = 2048
TM_MERGE = 512
TM_FFN = 512
TS_RNN = 64
RNN_BLOCKS_PER_STEP = 4
TN = 512
TF_FFN = 1536
TR_CAST = 256


def _cparams(*sem):
    return pltpu.CompilerParams(dimension_semantics=sem, vmem_limit_bytes=VMEM_LIMIT)


def _layer_norm(x, g, b):
    mu = jnp.mean(x, axis=-1, keepdims=True)
    xc = x - mu
    var = jnp.mean(xc * xc, axis=-1, keepdims=True)
    return xc * lax.rsqrt(var + LN_EPS) * g + b


def _gelu(x):
    k1 = -2.0 * GELU_C * LOG2E
    return x / (1.0 + jnp.exp2(x * (k1 + (k1 * GELU_A) * (x * x))))


def _softplus(x):
    return jnp.maximum(x, 0.0) + jnp.log1p(jnp.exp(-jnp.abs(x)))


def _const_spec(shape):
    nd = len(shape)
    return pl.BlockSpec(shape, lambda *_: (0,) * nd, pipeline_mode=pl.Buffered(1))


def _layer_spec(arr, l):
    tail = (0,) * (arr.ndim - 1)
    return pl.BlockSpec((None,) + arr.shape[1:], lambda *_: (l,) + tail, pipeline_mode=pl.Buffered(1))


def _slab_scratch(rows, cols):
    return pltpu.VMEM((cols // LANES, rows, LANES), F32)


def _put_sequence(slab_s, b, val):
    tt, n = val.shape
    for c in range(n // LANES):
        slab_s[c, pl.ds(b, tt, stride=SUBLANES), :] = val[:, c * LANES:(c + 1) * LANES]


def _get_sequence(slab_s, b, tt, c):
    return slab_s[c, pl.ds(b, tt, stride=SUBLANES), :]


def _slabs(slab_s):
    return jnp.concatenate([slab_s[c] for c in range(slab_s.shape[0])], axis=1)


def _ln_in_kernel(x_ref, g_ref, b_ref, h_ref, hb_ref, t_s):
    nb = x_ref.shape[0]
    for b in range(nb):
        _put_sequence(t_s, b, _layer_norm(x_ref[b], g_ref[...], b_ref[...]))
    y = _slabs(t_s)
    h_ref[...] = y
    hb_ref[...] = y.astype(BF16)


def _ln_in(x, g, b, tm):
    nb, seq, d = x.shape
    m = nb * seq
    tt = tm // nb
    return pl.pallas_call(
        _ln_in_kernel,
        grid=(m // tm,),
        in_specs=[pl.BlockSpec((nb, tt, d), lambda i: (0, i, 0)),
                  _const_spec((1, d)), _const_spec((1, d))],
        out_specs=[pl.BlockSpec((tm, d), lambda i: (i, 0)),
                   pl.BlockSpec((tm, d), lambda i: (i, 0))],
        out_shape=[jax.ShapeDtypeStruct((m, d), F32), jax.ShapeDtypeStruct((m, d), BF16)],
        scratch_shapes=[_slab_scratch(tm, d)],
        compiler_params=_cparams("parallel"),
        name="ln_in",
    )(x, g.reshape(1, d), b.reshape(1, d))


def _w_in_cast_kernel(w_ref, wa_ref, wz_ref):
    half = ROPE_DIM // 2
    mid = HEAD_DIM // 2
    pieces = ((0, half), (ROPE_DIM, mid + half), (half, ROPE_DIM), (mid + half, HEAD_DIM))
    na = wa_ref.shape[1]
    for hd in range(N_HEADS + N_KV_HEADS):
        x = w_ref[:, hd * HEAD_DIM:(hd + 1) * HEAD_DIM]
        y = jnp.concatenate([x[:, a:b] for a, b in pieces], axis=1)
        wa_ref[:, hd * HEAD_DIM:(hd + 1) * HEAD_DIM] = y.astype(BF16)
    rest = (N_HEADS + N_KV_HEADS) * HEAD_DIM
    wa_ref[:, rest:] = w_ref[:, rest:na].astype(BF16)
    wz_ref[...] = w_ref[:, na:].astype(BF16)


def _w_in_cast(w_in, na, tr):
    depth, d, n = w_in.shape
    return pl.pallas_call(
        _w_in_cast_kernel,
        grid=(depth, d // tr),
        in_specs=[pl.BlockSpec((None, tr, n), lambda l, i: (l, i, 0))],
        out_specs=[pl.BlockSpec((None, tr, na), lambda l, i: (l, i, 0)),
                   pl.BlockSpec((None, tr, n - na), lambda l, i: (l, i, 0))],
        out_shape=[jax.ShapeDtypeStruct((depth, d, na), BF16),
                   jax.ShapeDtypeStruct((depth, d, n - na), BF16)],
        compiler_params=_cparams("parallel", "parallel"),
        name="w_in_cast",
    )(w_in)


def _in_proj_kernel(h_ref, w_ref, c_ref, s_ref, o_ref, xr_ref, *z_scratch, n_qkv_tiles, n_rope_last):
    step = pl.program_id(1)
    j = step // 2
    is_xr = step % 2 == 1
    last = n_qkv_tiles - 1
    nb, tt, _ = o_ref.shape
    h = h_ref[...]

    @pl.when(is_xr)
    def _():
        xr_ref[...] = jnp.dot(h, w_ref[...], preferred_element_type=F32)

    @pl.when(jnp.logical_not(is_xr))
    def _():
        c, s = c_ref[...], s_ref[...]

        def rope(zc):
            return zc * c + pltpu.roll(zc, LANES // 2, 1) * s

        heads_per_dot = MXU_COLS // LANES

        def emit(z_s, ch, cc, rotary):
            for b in range(nb):
                zc = _get_sequence(z_s, b, tt, ch)
                o_ref[b, :, cc * LANES:(cc + 1) * LANES] = (rope(zc) if rotary else zc).astype(BF16)

        for cd, z_s in enumerate(z_scratch):
            z = jnp.dot(h, w_ref[:, cd * MXU_COLS:(cd + 1) * MXU_COLS], preferred_element_type=F32)
            for ch in range(heads_per_dot):
                z_s[ch] = z[:, ch * LANES:(ch + 1) * LANES]
            for ch in range(heads_per_dot):
                cc = cd * heads_per_dot + ch
                if cc < n_rope_last:
                    emit(z_s, ch, cc, True)
                else:
                    pl.when(j < last)(functools.partial(emit, z_s, ch, cc, True))
                    pl.when(j == last)(functools.partial(emit, z_s, ch, cc, False))


def _in_proj(hb, l, w, rope_c, rope_s, nb, n_qkv, tm, tn):
    m, d = hb.shape
    n = w.shape[2]
    seq = m // nb
    tt = tm // nb
    n_rope_cols = (N_HEADS + N_KV_HEADS) * HEAD_DIM
    assert n % tn == 0 and n_qkv % tn == 0 and tn % MXU_COLS == 0 and (n_qkv - tn) <= n_rope_cols <= n_qkv
    nq = n_qkv // tn
    assert (n - n_qkv) // tn == nq - 1
    n_rope_last = (n_rope_cols - (n_qkv - tn)) // LANES
    tab = pl.BlockSpec((tt, LANES), lambda i, j: (i, 0))
    return pl.pallas_call(
        functools.partial(_in_proj_kernel, n_qkv_tiles=nq, n_rope_last=n_rope_last),
        grid=(m // tm, n // tn),
        in_specs=[pl.BlockSpec((tm, d), lambda i, s: (i, 0)),
                  pl.BlockSpec((None, d, tn), lambda i, s: (l, 0, s // 2 + (s % 2) * nq)),
                  tab, tab],
        out_specs=[pl.BlockSpec((nb, tt, tn), lambda i, s: (0, i, s // 2)),
                   pl.BlockSpec((tm, tn), lambda i, s: (i, jnp.maximum(s - 1, 0) // 2))],
        out_shape=[jax.ShapeDtypeStruct((nb, seq, n_qkv), BF16),
                   jax.ShapeDtypeStruct((m, n - n_qkv), F32)],
        scratch_shapes=[_slab_scratch(tm, MXU_COLS)] * (tn // MXU_COLS),
        compiler_params=_cparams("parallel", "arbitrary"),
        name="in_proj",
    )(hb, w, rope_c, rope_s)


def _rnn_kernel(xf_ref, xb_ref, cw_ref, cb_ref, wg_ref, ba_ref, bx_ref, lam_ref,
                hf_ref, hb_ref, xsf_s, xsb_s, af_s, uf_s, ab_s, ub_s, carry_s, *, nb):
    i = pl.program_id(1)
    rows, bw = hf_ref.shape
    blk = wg_ref.shape[2]
    ts = rows // nb
    halo = (RNN_CONV_W - 1) * nb

    @pl.when(i == 0)
    def _():
        xsf_s[0:halo, :] = jnp.zeros((halo, bw), F32)
        xsb_s[rows:rows + halo, :] = jnp.zeros((halo, bw), F32)
        carry_s[...] = jnp.zeros_like(carry_s)

    def gates(d, half_x, start, a_s, u_s):
        xcb = half_x.astype(BF16)
        gs = [jnp.dot(xcb[:, sb * blk:(sb + 1) * blk], wg_ref[d, sb], preferred_element_type=F32)
              for sb in range(bw // blk)]
        g_a = jnp.concatenate([g[:, :blk] for g in gs], axis=1)
        g_x = jnp.concatenate([g[:, blk:] for g in gs], axis=1)
        t_a = jnp.tanh(g_a + ba_ref[d] * 0.5)
        t_x = jnp.tanh(g_x + bx_ref[d] * 0.5)
        half_c = (-0.5 * RG_C) * _softplus(-lam_ref[d])
        log_a = half_c + half_c * t_a
        a = jnp.exp(log_a)
        w = jnp.tanh(log_a) * (-1.0 - a * a)
        mult = jnp.where(w > 0.0, w * lax.rsqrt(w), 0.0)
        xg = half_x + half_x * t_x
        a_s[...] = a
        u_s[...] = xg * mult
        return xg[start, :]

    xsf_s[halo:halo + rows, :] = xf_ref[...]
    yf = cb_ref[0] * 0.5 + (cw_ref[0, 0] * 0.5) * xsf_s[halo:halo + rows, :]
    for k in range(1, RNN_CONV_W):
        yf = yf + (cw_ref[0, k] * 0.5) * xsf_s[halo - k * nb:halo - k * nb + rows, :]
    xsf_s[0:halo, :] = xsf_s[rows:rows + halo, :]
    first_f = gates(0, yf, slice(0, nb), af_s, uf_s)

    xsb_s[0:rows, :] = xb_ref[...]
    yb = cb_ref[1] * 0.5 + (cw_ref[1, 0] * 0.5) * xsb_s[0:rows, :]
    for k in range(1, RNN_CONV_W):
        yb = yb + (cw_ref[1, k] * 0.5) * xsb_s[k * nb:k * nb + rows, :]
    xsb_s[rows:rows + halo, :] = xsb_s[0:halo, :]
    first_b = gates(1, yb, slice(rows - nb, rows), ab_s, ub_s)

    @pl.when(i == 0)
    def _():
        uf_s[0:nb, :] = first_f
        ub_s[rows - nb:rows, :] = first_b

    def step(tt, carry):
        cf, cb = carry
        rf = pl.ds(pl.multiple_of(tt * nb, nb), nb)
        rb = pl.ds(pl.multiple_of((ts - 1 - tt) * nb, nb), nb)
        cf = af_s[rf, :] * cf + uf_s[rf, :]
        cb = ab_s[rb, :] * cb + ub_s[rb, :]
        hf_ref[rf, :] = cf
        hb_ref[rb, :] = cb
        return cf, cb

    cf, cb = lax.fori_loop(0, ts, step, (carry_s[0], carry_s[1]), unroll=8)
    carry_s[0] = cf
    carry_s[1] = cb


def _rnn(xr, l, conv_w, conv_b, wg, b_a, b_x, lam, nb, ts):
    m, d = xr.shape
    blk = d // N_RNN_BLOCKS
    bw = RNN_BLOCKS_PER_STEP * blk
    rows = ts * nb
    nt = m // rows
    halo = (RNN_CONV_W - 1) * nb
    xs = pltpu.VMEM((rows + halo, bw), F32)
    au = pltpu.VMEM((rows, bw), F32)
    vec = lambda: pl.BlockSpec((None, 2, 1, bw), lambda c, i: (l, 0, 0, c))
    return pl.pallas_call(
        functools.partial(_rnn_kernel, nb=nb),
        grid=(d // bw, nt),
        in_specs=[pl.BlockSpec((rows, bw), lambda c, i: (i, c)),
                  pl.BlockSpec((rows, bw), lambda c, i: (nt - 1 - i, c)),
                  pl.BlockSpec((None, 2, RNN_CONV_W, 1, bw), lambda c, i: (l, 0, 0, 0, c)),
                  vec(),
                  pl.BlockSpec((None, 2, RNN_BLOCKS_PER_STEP, blk, 2 * blk), lambda c, i: (l, 0, c, 0, 0)),
                  vec(), vec(), vec()],
        out_specs=[pl.BlockSpec((rows, bw), lambda c, i: (i, c)),
                   pl.BlockSpec((rows, bw), lambda c, i: (nt - 1 - i, c))],
        out_shape=[jax.ShapeDtypeStruct((m, d), F32)] * 2,
        scratch_shapes=[xs, xs, au, au, au, au, pltpu.VMEM((2, nb, bw), F32)],
        compiler_params=_cparams("parallel", "arbitrary"),
        name="rg_lru",
    )(xr, xr, conv_w, conv_b, wg, b_a, b_x, lam)


def _attn_kernel(sink_ref, q_ref, k_ref, v_ref, o_ref, kp_s, vp_s, s_s, p_s, *, l):
    hk = pl.program_id(1)
    seq = q_ref.shape[1]
    blk = WINDOW
    nblk = seq // blk
    span = 3 * blk
    c1 = HEAD_DIM ** -0.5 * LOG2E

    kp_s[0] = jnp.zeros((HEAD_DIM, blk), BF16)
    for jb in range(nblk):
        kp_s[jb + 1] = k_ref[0, jb * blk:(jb + 1) * blk, :].T
    kp_s[nblk + 1] = jnp.zeros((HEAD_DIM, blk), BF16)
    vp_s[0:blk, 0:HEAD_DIM] = jnp.zeros((blk, HEAD_DIM), BF16)
    vp_s[blk:blk + seq, 0:HEAD_DIM] = v_ref[0]
    vp_s[blk + seq:2 * blk + seq, 0:HEAD_DIM] = jnp.zeros((blk, HEAD_DIM), BF16)
    vp_s[:, HEAD_DIM:] = jnp.ones((seq + 2 * blk, HEAD_DIM), BF16)

    qi = lax.broadcasted_iota(jnp.int32, (blk, blk), 0)
    kj = lax.broadcasted_iota(jnp.int32, (blk, blk), 1)
    bias_prev = jnp.where(kj >= qi, 0.0, NEG_INF)
    bias_next = jnp.where(kj <= qi, 0.0, NEG_INF)

    def logits(n, slot):
        q0 = pl.multiple_of(n * blk, blk)
        qs = jnp.concatenate(
            [q_ref[0, pl.ds(q0, blk), g * HEAD_DIM:(g + 1) * HEAD_DIM] for g in range(Q_GROUP)], axis=0)
        kw = jnp.concatenate([kp_s[n], kp_s[n + 1], kp_s[n + 2]], axis=1)
        s_s[slot] = jnp.dot(qs, kw, preferred_element_type=F32)

    def softmax(n, slot):
        bp = jnp.where(n == 0, NEG_INF, bias_prev)
        bn = jnp.where(n == nblk - 1, NEG_INF, bias_next)
        m2s = []
        for g in range(Q_GROUP):
            r = slice(g * blk, (g + 1) * blk)
            s_prev = s_s[slot, r, 0:blk] + bp
            s_cur = s_s[slot, r, blk:2 * blk]
            s_next = s_s[slot, r, 2 * blk:span] + bn
            m_raw = jnp.max(jnp.maximum(jnp.maximum(s_prev, s_cur), s_next), axis=-1, keepdims=True)
            m2 = jnp.maximum(m_raw * c1, sink_ref[l, hk * Q_GROUP + g] * LOG2E)
            p_s[slot, r, 0:blk] = jnp.exp2(s_prev * c1 - m2).astype(BF16)
            p_s[slot, r, blk:2 * blk] = jnp.exp2(s_cur * c1 - m2).astype(BF16)
            p_s[slot, r, 2 * blk:span] = jnp.exp2(s_next * c1 - m2).astype(BF16)
            m2s.append(m2)
        return tuple(m2s)

    def values(n, slot, m2s):
        q0 = n * blk if isinstance(n, int) else pl.multiple_of(n * blk, blk)
        oe = jnp.dot(p_s[slot], vp_s[pl.ds(q0, span), :], preferred_element_type=F32)
        for g in range(Q_GROUP):
            r = slice(g * blk, (g + 1) * blk)
            den = oe[r, HEAD_DIM:] + jnp.exp2(sink_ref[l, hk * Q_GROUP + g] * LOG2E - m2s[g])
            o_ref[0, pl.ds(q0, blk), g * HEAD_DIM:(g + 1) * HEAD_DIM] = (oe[r, :HEAD_DIM] / den).astype(BF16)

    logits(0, 0)
    p_s[1] = jnp.zeros(p_s.shape[1:], BF16)

    def body(i, m_prev):
        n = 2 * i
        logits(n + 1, 1)
        m_even = softmax(n, 0)
        values(jnp.maximum(n - 1, 0), 1, m_prev)
        logits(jnp.minimum(n + 2, nblk - 1), 0)
        m_odd = softmax(n + 1, 1)
        values(n, 0, m_even)
        return m_odd

    assert nblk % 2 == 0
    m_init = tuple(jnp.zeros((blk, 1), F32) for _ in range(Q_GROUP))
    m_last = lax.fori_loop(0, nblk // 2, body, m_init)
    values(nblk - 1, 1, m_last)


def _attention(qkv3, l, sink):
    nb, seq, _ = qkv3.shape
    gw = Q_GROUP * HEAD_DIM
    k_blk0 = N_HEADS
    v_blk0 = N_HEADS + N_KV_HEADS
    rows = Q_GROUP * WINDOW
    return pl.pallas_call(
        functools.partial(_attn_kernel, l=l),
        grid=(nb, N_KV_HEADS),
        in_specs=[pl.BlockSpec(memory_space=pltpu.SMEM),
                  pl.BlockSpec((1, seq, gw), lambda b, h: (b, 0, h)),
                  pl.BlockSpec((1, seq, HEAD_DIM), lambda b, h: (b, 0, k_blk0 + h)),
                  pl.BlockSpec((1, seq, HEAD_DIM), lambda b, h: (b, 0, v_blk0 + h))],
        out_specs=pl.BlockSpec((1, seq, gw), lambda b, h: (b, 0, h)),
        out_shape=jax.ShapeDtypeStruct((nb, seq, N_HEADS * HEAD_DIM), BF16),
        scratch_shapes=[pltpu.VMEM((seq // WINDOW + 2, HEAD_DIM, WINDOW), BF16),
                        pltpu.VMEM((seq + 2 * WINDOW, 2 * HEAD_DIM), BF16),
                        pltpu.VMEM((2, rows, 3 * WINDOW), F32),
                        pltpu.VMEM((2, rows, 3 * WINDOW), BF16)],
        compiler_params=_cparams("parallel", "parallel"),
        name="swa_attention",
    )(sink, qkv3, qkv3, qkv3)


def _merge_kernel(hb_ref, attn_ref, hf_ref, hbk_ref, h_ref,
                  wz_ref, wpa_ref, wpr_ref, wo_ref, g_ref, b_ref, o_ref, ob_ref, at_s, *, alpha):
    d = h_ref.shape[1]
    nb = attn_ref.shape[0]
    z = jnp.dot(hb_ref[...], wz_ref[...], preferred_element_type=F32)
    rnn = (hf_ref[...] + hbk_ref[...]) * _gelu(z[:, :d])
    for b in range(nb):
        _put_sequence(at_s, b, attn_ref[b].astype(F32))
    pa = jnp.dot(_slabs(at_s).astype(BF16), wpa_ref[...], preferred_element_type=F32)
    pr = jnp.dot(rnn.astype(BF16), wpr_ref[...], preferred_element_type=F32)
    merged = jax.nn.sigmoid(z[:, d:2 * d]) * pa + jax.nn.sigmoid(z[:, 2 * d:]) * pr
    o = jnp.dot(merged.astype(BF16), wo_ref[...], preferred_element_type=F32)
    y = _layer_norm(alpha * h_ref[...] + o, g_ref[...], b_ref[...])
    o_ref[...] = y
    ob_ref[...] = y.astype(BF16)


def _merge(hb, attn3, hf, hbk, h, l, wz, wpa, wpr, wo, g, b, alpha, tm):
    m, d = h.shape
    nb = attn3.shape[0]
    row = pl.BlockSpec((tm, d), lambda i: (i, 0))
    return pl.pallas_call(
        functools.partial(_merge_kernel, alpha=alpha),
        grid=(m // tm,),
        in_specs=[row, pl.BlockSpec((nb, tm // nb, d), lambda i: (0, i, 0)), row, row, row]
                 + [_layer_spec(a, l) for a in (wz, wpa, wpr, wo, g, b)],
        out_specs=[row, row],
        out_shape=[jax.ShapeDtypeStruct((m, d), F32), jax.ShapeDtypeStruct((m, d), BF16)],
        scratch_shapes=[_slab_scratch(tm, d)],
        compiler_params=_cparams("parallel"),
        name="merge_out_ln",
    )(hb, attn3, hf, hbk, h, wz, wpa, wpr, wo, g, b)


def _ffn_kernel(hb_ref, hprev_ref, hnext_ref, h_ref, p_ref, wu_ref, wd_ref, cw_ref, cb_ref,
                wpg_ref, wple_ref, g_ref, b_ref, *rest, alpha, tf, nb, batch_major_out):
    if batch_major_out:
        o_ref, p_s, y_s = rest
    else:
        o_ref, ob_ref, p_s = rest
    i = pl.program_id(0)
    tm = hb_ref.shape[0]
    dff = wd_ref.shape[0]
    halo = nb
    hb = hb_ref[...]
    hp = hprev_ref[hprev_ref.shape[0] - nb:, :]
    hn = hnext_ref[:nb, :]
    hp = jnp.where(i == 0, jnp.zeros_like(hp), hp)
    hn = jnp.where(i == pl.num_programs(0) - 1, jnp.zeros_like(hn), hn)
    lhs = jnp.concatenate([hp, hb, hn], axis=0)
    acc = None
    for j in range(dff // tf):
        cs = slice(j * tf, (j + 1) * tf)
        gate = jnp.dot(lhs, wu_ref[:, cs], preferred_element_type=F32)
        val = jnp.dot(hb, wu_ref[:, dff + j * tf:dff + (j + 1) * tf], preferred_element_type=F32)
        conv = (cw_ref[0:1, cs] * gate[halo - nb:halo - nb + tm] + cw_ref[1:2, cs] * gate[halo:halo + tm]
                + cw_ref[2:3, cs] * gate[halo + nb:halo + nb + tm] + cb_ref[:, cs])
        act = (_gelu(conv) * val).astype(BF16)
        pc = jnp.dot(act, wd_ref[cs, :], preferred_element_type=F32)
        acc = pc if acc is None else acc + pc

    for b in range(nb):
        _put_sequence(p_s, b, p_ref[b])
    ple = (jax.nn.sigmoid(jnp.dot(hb, wpg_ref[...], preferred_element_type=F32))
           * jnp.dot(_slabs(p_s).astype(BF16), wple_ref[...], preferred_element_type=F32))
    y = _layer_norm(alpha * h_ref[...] + acc + ple, g_ref[...], b_ref[...])
    if batch_major_out:
        for c in range(y_s.shape[0]):
            y_s[c] = y[:, c * LANES:(c + 1) * LANES]
        for b in range(nb):
            for c in range(y_s.shape[0]):
                o_ref[b, :, c * LANES:(c + 1) * LANES] = _get_sequence(y_s, b, tm // nb, c)
    else:
        o_ref[...] = y
        ob_ref[...] = y.astype(BF16)


def _ffn(h, hb, l, p4, w_up, w_down, cw, cb, wpg, wple, g, b, alpha, tm, tf, batch_major_out):
    m, d = h.shape
    _, nb, seq, dple = p4.shape
    tt = tm // nb
    halo = BF16_ROWS
    per = tm // halo
    last = m // halo - 1
    row = pl.BlockSpec((tm, d), lambda i: (i, 0))
    if batch_major_out:
        out_specs = pl.BlockSpec((nb, tt, d), lambda i: (0, i, 0))
        out_shape = jax.ShapeDtypeStruct((nb, seq, d), F32)
        scratch = [_slab_scratch(tm, dple), _slab_scratch(tm, d)]
    else:
        out_specs = [row, row]
        out_shape = [jax.ShapeDtypeStruct((m, d), F32), jax.ShapeDtypeStruct((m, d), BF16)]
        scratch = [_slab_scratch(tm, dple)]
    return pl.pallas_call(
        functools.partial(_ffn_kernel, alpha=alpha, tf=tf, nb=nb, batch_major_out=batch_major_out),
        grid=(m // tm,),
        in_specs=[row,
                  pl.BlockSpec((halo, d), lambda i: (jnp.maximum(i * per - 1, 0), 0)),
                  pl.BlockSpec((halo, d), lambda i: (jnp.minimum((i + 1) * per, last), 0)),
                  row,
                  pl.BlockSpec((None, nb, tt, dple), lambda i: (l, 0, i, 0))]
                 + [_layer_spec(a, l) for a in (w_up, w_down, cw, cb, wpg, wple, g, b)],
        out_specs=out_specs,
        out_shape=out_shape,
        scratch_shapes=scratch,
        compiler_params=_cparams("parallel"),
        name="ffn_ple_ln",
    )(hb, hb, hb, h, p4, w_up, w_down, cw, cb, wpg, wple, g, b)


def _rope_tables(seq):
    half = ROPE_DIM // 2
    mid = HEAD_DIM // 2
    pos = jnp.arange(seq, dtype=F32)
    inv = ROPE_THETA ** (-jnp.arange(0, ROPE_DIM, 2, dtype=F32) / ROPE_DIM)
    ang = pos[:, None] * inv[None, :]
    cos, sin = jnp.cos(ang), jnp.sin(ang)
    ones = jnp.ones((seq, mid - half), F32)
    zeros = jnp.zeros((seq, mid - half), F32)
    c = jnp.concatenate([cos, ones, cos, ones], axis=1)
    s = jnp.concatenate([-sin, zeros, sin, zeros], axis=1)
    return c, s


def kernel(x, p, ln_in_g, ln_in_b, w_in, attn_sink, rnn_conv_w, rnn_conv_b, rg_w_a, rg_b_a, rg_w_x, rg_b_x, rg_lambda, w_proj_attn, w_proj_rnn, w_out, ln1_g, ln1_b, w_up, ffn_conv_w, ffn_conv_b, w_down, w_ple, w_ple_gate, ln2_g, ln2_b):
    nb, seq, d = x.shape
    depth = w_in.shape[0]
    m = nb * seq
    alpha = float((2 * depth) ** 0.25)
    n_qkv = (N_HEADS + 2 * N_KV_HEADS) * HEAD_DIM
    assert nb == SUBLANES and d == N_HEADS * HEAD_DIM and seq % WINDOW == 0 and seq >= 3 * WINDOW

    tm_ln = min(TM_LN, m)
    tm_proj = min(TM_PROJ, m)
    tm_merge = min(TM_MERGE, m)
    tm_ffn = min(TM_FFN, m)
    ts_rnn = min(TS_RNN, seq)

    w_a, w_z = _w_in_cast(w_in, n_qkv + d, min(TR_CAST, d))
    w_g = jnp.concatenate([rg_w_a, rg_w_x], axis=-1).astype(BF16)
    w_pa, w_pr, w_o = w_proj_attn.astype(BF16), w_proj_rnn.astype(BF16), w_out.astype(BF16)
    w_u, w_d = w_up.astype(BF16), w_down.astype(BF16)
    w_pg, w_pl = w_ple_gate.astype(BF16), w_ple.astype(BF16)

    row = lambda a: a.reshape(a.shape[:-1] + (1, a.shape[-1]))
    conv_w, conv_b = row(rnn_conv_w), row(rnn_conv_b)
    b_a, b_x, lam = row(rg_b_a), row(rg_b_x), row(rg_lambda)
    g1, b1, g2, b2, ffn_cb = row(ln1_g), row(ln1_b), row(ln2_g), row(ln2_b), row(ffn_conv_b)

    rope_c, rope_s = _rope_tables(seq)
    h, hb = _ln_in(x, ln_in_g, ln_in_b, tm_ln)

    for l in range(depth):
        qkv, xr = _in_proj(hb, l, w_a, rope_c, rope_s, nb, n_qkv, tm_proj, TN)
        hf, hbk = _rnn(xr, l, conv_w, conv_b, w_g, b_a, b_x, lam, nb, ts_rnn)
        attn = _attention(qkv, l, attn_sink)
        h, hb = _merge(hb, attn, hf, hbk, h, l, w_z, w_pa, w_pr, w_o, g1, b1, alpha, tm_merge)
        out = _ffn(h, hb, l, p, w_u, w_d, ffn_conv_w, ffn_cb, w_pg, w_pl, g2, b2,
                   alpha, tm_ffn, TF_FFN, batch_major_out=(l == depth - 1))
        if l < depth - 1:
            h, hb = out
    return out
```
